```python
import math
import jax, jax.numpy as jnp
from jax import lax
import numpy as np

D_MODEL = 1024
BATCH = 8
SEQ = 4096
DEPTH = 2
DEC_BATCH = 8
DEC_SEQ = 8192
PAST_LEN = 128

N_HEADS = 8
Q_LORA = 384
KV_LORA = 256
QK_NOPE = 64
QK_ROPE = 32
V_HEAD = 64
MLA_W = N_HEADS * V_HEAD
ROPE_THETA = 10000.0
Q_BLOCK = 128
GM_GROUPS = 4
GM_GROUP_CH = 128
GM_W = GM_GROUPS * GM_GROUP_CH
GM_CHUNK = 128
MEM_LEN = 256
MEM_HEADS = 4
MEM_HEAD_DIM = 128
MEM_W = MEM_HEADS * MEM_HEAD_DIM
N_BRANCH = 3
D_FF = 2816
CONV_W = 3
LN_EPS = 1e-5
RMS_EPS = 1e-6
ALPHA = (2 * DEPTH) ** 0.25
BETA = (8 * DEPTH) ** -0.25
IN_SPLITS = (Q_LORA, KV_LORA, QK_ROPE, GM_W, GM_W, MEM_W, N_BRANCH * D_MODEL)
IN_WIDTH = Q_LORA + KV_LORA + QK_ROPE + 2 * GM_W + MEM_W + N_BRANCH * D_MODEL

kernel_name = 'hybrid_mla_gmlp_mem_encoder'


def _split_offsets():
    offs = []
    t = 0
    for w in IN_SPLITS[:-1]:
        t += w
        offs.append(t)
    return offs


def layer_norm(x, g, b):
    xf = x.astype(jnp.float32)
    mu = jnp.mean(xf, axis=-1, keepdims=True)
    var = jnp.mean(jnp.square(xf - mu), axis=-1, keepdims=True)
    y = (xf - mu) * lax.rsqrt(var + LN_EPS) * g.astype(jnp.float32) + b.astype(jnp.float32)
    return y.astype(x.dtype)


def rms_norm(x, g):
    xf = x.astype(jnp.float32)
    y = xf * lax.rsqrt(jnp.mean(jnp.square(xf), axis=-1, keepdims=True) + RMS_EPS) * g.astype(jnp.float32)
    return y.astype(x.dtype)


def rope_tables(seq_len):
    pos = jnp.arange(seq_len, dtype=jnp.float32)
    inv = ROPE_THETA ** (-jnp.arange(0, QK_ROPE, 2, dtype=jnp.float32) / QK_ROPE)
    ang = pos[:, None] * inv[None, :]
    return jnp.cos(ang), jnp.sin(ang)


def apply_rope(x, cos, sin):
    x1, x2 = jnp.split(x, 2, axis=-1)
    return jnp.concatenate([x1 * cos - x2 * sin, x1 * sin + x2 * cos], axis=-1)


def mla_branch(c_q, c_kv, k_r, cos, sin, g_q, w_uq, g_kv, w_ukv):
    b, s, _ = c_q.shape
    q = (rms_norm(c_q, g_q) @ w_uq).reshape(b, s, N_HEADS, QK_NOPE + QK_ROPE)
    q_nope, q_rope = q[..., :QK_NOPE], q[..., QK_NOPE:]
    q_rope = apply_rope(q_rope, cos[:, None, :], sin[:, None, :])
    kv = (rms_norm(c_kv, g_kv) @ w_ukv).reshape(b, s, N_HEADS, QK_NOPE + V_HEAD)
    k_nope, v = kv[..., :QK_NOPE], kv[..., QK_NOPE:]
    k_rope = apply_rope(k_r, cos, sin)
    scale = (QK_NOPE + QK_ROPE) ** -0.5
    nb = s // Q_BLOCK
    qn = q_nope.reshape(b, nb, Q_BLOCK, N_HEADS, QK_NOPE).swapaxes(0, 1)
    qr = q_rope.reshape(b, nb, Q_BLOCK, N_HEADS, QK_ROPE).swapaxes(0, 1)

    def block(args):
        qn_b, qr_b = args
        sc = jnp.einsum('bqhd,bkhd->bhqk', qn_b, k_nope) + jnp.einsum('bqhr,bkr->bhqk', qr_b, k_rope)
        p = jax.nn.softmax(sc.astype(jnp.float32) * scale, axis=-1).astype(v.dtype)
        return jnp.einsum('bhqk,bkhd->bqhd', p, v)

    o = lax.map(block, (qn, qr))
    return o.swapaxes(0, 1).reshape(b, s, MLA_W)


def gmlp_branch(z_u, z_v, gm_ln_g, gm_ln_b, w_s, b_s):
    b, s, _ = z_u.shape
    u = jax.nn.gelu(z_u)
    v = layer_norm(jax.nn.gelu(z_v), gm_ln_g, gm_ln_b)
    nc = s // GM_CHUNK
    v = v.reshape(b, nc, GM_CHUNK, GM_GROUPS, GM_GROUP_CH)
    sp = jnp.einsum('gpq,bnqgc->bnpgc', w_s, v) + b_s.T[None, None, :, :, None]
    return u * sp.reshape(b, s, GM_W)


def mem_branch(q_in, mem, w_mem_kv):
    b, s, _ = q_in.shape
    m = mem.shape[1]
    q = q_in.reshape(b, s, MEM_HEADS, MEM_HEAD_DIM)
    kv = (mem @ w_mem_kv).reshape(b, m, 2, MEM_HEADS, MEM_HEAD_DIM)
    k, v = kv[:, :, 0], kv[:, :, 1]
    sc = jnp.einsum('bqhd,bkhd->bhqk', q, k).astype(jnp.float32) * (MEM_HEAD_DIM ** -0.5)
    p = jax.nn.softmax(sc, axis=-1).astype(v.dtype)
    return jnp.einsum('bhqk,bkhd->bqhd', p, v).reshape(b, s, MEM_W)


def conv_ffn(x, w_ffn_in, conv_w, conv_b, w_ffn_out):
    s = x.shape[1]
    h = x @ w_ffn_in
    pad = CONV_W // 2
    hp = jnp.pad(h, ((0, 0), (pad, pad), (0, 0)))
    hc = conv_b
    for t in range(CONV_W):
        hc = hc + hp[:, t:t + s] * conv_w[t]
    a, g = jnp.split(hc, 2, axis=-1)
    return (jax.nn.silu(a) * g) @ w_ffn_out


def trunk(x, mem, w_in, g_q, w_uq, g_kv, w_ukv, gm_ln_g, gm_ln_b, w_s, b_s, w_mem_kv,
          b_gate, w_br_mla, w_br_gmlp, w_br_mem, w_o, ln1_g, ln1_b,
          w_ffn_in, conv_w, conv_b, w_ffn_out, ln2_g, ln2_b):
    b, s, d = x.shape
    cos, sin = rope_tables(s)
    cos = cos.astype(x.dtype)
    sin = sin.astype(x.dtype)
    offs = _split_offsets()
    for l in range(DEPTH):
        z = x @ w_in[l]
        c_q, c_kv, k_r, z_u, z_v, q_mem, gate_pre = jnp.split(z, offs, axis=-1)
        o_a = mla_branch(c_q, c_kv, k_r, cos, sin, g_q[l], w_uq[l], g_kv[l], w_ukv[l])
        o_b = gmlp_branch(z_u, z_v, gm_ln_g[l], gm_ln_b[l], w_s[l], b_s[l])
        o_c = mem_branch(q_mem, mem, w_mem_kv[l])
        gates = jax.nn.sigmoid(gate_pre + b_gate[l]).reshape(b, s, N_BRANCH, d)
        merged = (gates[:, :, 0] * (o_a @ w_br_mla[l])
                  + gates[:, :, 1] * (o_b @ w_br_gmlp[l])
                  + gates[:, :, 2] * (o_c @ w_br_mem[l]))
        x = layer_norm(ALPHA * x + merged @ w_o[l], ln1_g[l], ln1_b[l])
        x = layer_norm(ALPHA * x + conv_ffn(x, w_ffn_in[l], conv_w[l], conv_b[l], w_ffn_out[l]),
                       ln2_g[l], ln2_b[l])
    return x


def setup_inputs(seed: int = 0) -> dict:
    key = jax.random.key(seed)
    ks = jax.random.split(key, 32)

    def nrm(k, shape, scale):
        return jax.random.normal(k, shape, jnp.float32) * scale

    def gain(k, shape):
        return 1.0 + 0.02 * jax.random.normal(k, shape, jnp.float32)

    return {
        'x_prompt': nrm(ks[0], (BATCH, SEQ, D_MODEL), 1.0),
        'x_sample': nrm(ks[1], (DEC_BATCH, DEC_SEQ, D_MODEL), 1.0),
        'mem_prompt': nrm(ks[2], (BATCH, MEM_LEN, D_MODEL), 1.0),
        'mem_sample': nrm(ks[3], (DEC_BATCH, MEM_LEN, D_MODEL), 1.0),
        'w_in': nrm(ks[4], (DEPTH, D_MODEL, IN_WIDTH), D_MODEL ** -0.5),
        'g_q': gain(ks[5], (DEPTH, Q_LORA)),
        'w_uq': nrm(ks[6], (DEPTH, Q_LORA, N_HEADS * (QK_NOPE + QK_ROPE)), Q_LORA ** -0.5),
        'g_kv': gain(ks[7], (DEPTH, KV_LORA)),
        'w_ukv': nrm(ks[8], (DEPTH, KV_LORA, N_HEADS * (QK_NOPE + V_HEAD)), KV_LORA ** -0.5),
        'gm_ln_g': gain(ks[9], (DEPTH, GM_W)),
        'gm_ln_b': nrm(ks[10], (DEPTH, GM_W), 0.02),
        'w_s': nrm(ks[11], (DEPTH, GM_GROUPS, GM_CHUNK, GM_CHUNK), GM_CHUNK ** -0.5),
        'b_s': gain(ks[12], (DEPTH, GM_GROUPS, GM_CHUNK)),
        'w_mem_kv': nrm(ks[13], (DEPTH, D_MODEL, 2 * MEM_W), D_MODEL ** -0.5),
        'b_gate': nrm(ks[14], (DEPTH, N_BRANCH * D_MODEL), 0.02),
        'w_br_mla': nrm(ks[15], (DEPTH, MLA_W, D_MODEL), BETA * MLA_W ** -0.5),
        'w_br_gmlp': nrm(ks[16], (DEPTH, GM_W, D_MODEL), BETA * GM_W ** -0.5),
        'w_br_mem': nrm(ks[17], (DEPTH, MEM_W, D_MODEL), BETA * MEM_W ** -0.5),
        'w_o': nrm(ks[18], (DEPTH, D_MODEL, D_MODEL), BETA * D_MODEL ** -0.5),
        'ln1_g': gain(ks[19], (DEPTH, D_MODEL)),
        'ln1_b': nrm(ks[20], (DEPTH, D_MODEL), 0.02),
        'w_ffn_in': nrm(ks[21], (DEPTH, D_MODEL, 2 * D_FF), D_MODEL ** -0.5),
        'conv_w': nrm(ks[22], (DEPTH, CONV_W, 2 * D_FF), CONV_W ** -0.5),
        'conv_b': nrm(ks[23], (DEPTH, 2 * D_FF), 0.02),
        'w_ffn_out': nrm(ks[24], (DEPTH, D_FF, D_MODEL), BETA * D_FF ** -0.5),
        'ln2_g': gain(ks[25], (DEPTH, D_MODEL)),
        'ln2_b': nrm(ks[26], (DEPTH, D_MODEL), 0.02),
    }


def reference(x_prompt, x_sample, mem_prompt, mem_sample, w_in, g_q, w_uq, g_kv, w_ukv,
              gm_ln_g, gm_ln_b, w_s, b_s, w_mem_kv, b_gate, w_br_mla, w_br_gmlp, w_br_mem,
              w_o, ln1_g, ln1_b, w_ffn_in, conv_w, conv_b, w_ffn_out, ln2_g, ln2_b):
    y_prompt = trunk(x_prompt, mem_prompt, w_in, g_q, w_uq, g_kv, w_ukv, gm_ln_g, gm_ln_b, w_s, b_s,
                     w_mem_kv, b_gate, w_br_mla, w_br_gmlp, w_br_mem, w_o, ln1_g, ln1_b,
                     w_ffn_in, conv_w, conv_b, w_ffn_out, ln2_g, ln2_b)
    y_sample = trunk(x_sample, mem_sample, w_in, g_q, w_uq, g_kv, w_ukv, gm_ln_g, gm_ln_b, w_s, b_s,
                     w_mem_kv, b_gate, w_br_mla, w_br_gmlp, w_br_mem, w_o, ln1_g, ln1_b,
                     w_ffn_in, conv_w, conv_b, w_ffn_out, ln2_g, ln2_b)
    return (y_prompt, y_sample)
```

```python
import functools
import math

import jax
import jax.numpy as jnp
from jax import lax
from jax.experimental import pallas as pl
from jax.experimental.pallas import tpu as pltpu

D_MODEL = 1024
DEPTH = 2
N_HEADS = 8
Q_LORA = 384
KV_LORA = 256
QK_NOPE = 64
QK_ROPE = 32
V_HEAD = 64
ROPE_THETA = 10000.0
GM_GROUPS = 4
GM_CHUNK = 128
GM_W = 512
MEM_HEADS = 4
MEM_HEAD_DIM = 128
MEM_W = 512
D_FF = 2816
LN_EPS = 1e-5
RMS_EPS = 1e-6
ALPHA = (2 * DEPTH) ** 0.25
LOG2E = 1.4426950408889634

HEAD_PAD = 128
ROPE_LANE = QK_NOPE
ONES_ROWS = 16

C_Q0, C_KV0, C_KR0, C_UV0, C_QM0, C_G0 = 0, 384, 640, 768, 1792, 2304
C_END = C_G0 + 3 * D_MODEL

VMEM_LIMIT = 58 * 1024 * 1024

BF = jnp.bfloat16
F32 = jnp.float32


def _dot(a, b):
    return jnp.dot(a, b, preferred_element_type=F32)


def _dot_nt(a, b):
    return lax.dot_general(a, b, (((1,), (1,)), ((), ())), preferred_element_type=F32)


def _dot_tn(a, b):
    return lax.dot_general(a, b, (((0,), (0,)), ((), ())), preferred_element_type=F32)


def _layer_norm(x, g, b):
    mu = jnp.mean(x, axis=-1, keepdims=True)
    xc = x - mu
    var = jnp.mean(xc * xc, axis=-1, keepdims=True)
    return xc * lax.rsqrt(var + LN_EPS) * g + b


def _rms_norm(x, g):
    return x * lax.rsqrt(jnp.mean(x * x, axis=-1, keepdims=True) + RMS_EPS) * g


def _rope(x, ta, tb):
    return x * ta + pltpu.roll(x, HEAD_PAD - QK_ROPE, 1) * tb


def _const_spec(shape):
    nd = len(shape)
    return pl.BlockSpec(shape, lambda *_: (0,) * nd, pipeline_mode=pl.Buffered(1))


def _params(n_axes):
    return pltpu.CompilerParams(dimension_semantics=("arbitrary",) * n_axes,
                                vmem_limit_bytes=VMEM_LIMIT)


def _mem_kv_kernel(mem_ref, w_ref, km_ref, vm_ref):
    kv = _dot(mem_ref[0].astype(BF), w_ref[...])
    km_ref[0] = (kv[:, :MEM_W] * (MEM_HEAD_DIM ** -0.5 * LOG2E)).astype(BF)
    vm_ref[0] = kv[:, MEM_W:].astype(BF)


def _mem_kv(mem, w_bf):
    b, m, _ = mem.shape
    return pl.pallas_call(
        _mem_kv_kernel,
        grid=(b,),
        in_specs=[pl.BlockSpec((1, m, D_MODEL), lambda i: (i, 0, 0)),
                  _const_spec((D_MODEL, 2 * MEM_W))],
        out_specs=[pl.BlockSpec((1, m, MEM_W), lambda i: (i, 0, 0)),
                   pl.BlockSpec((1, m, MEM_W), lambda i: (i, 0, 0))],
        out_shape=[jax.ShapeDtypeStruct((b, m, MEM_W), BF),
                   jax.ShapeDtypeStruct((b, m, MEM_W), BF)],
        compiler_params=_params(1),
        name="mem_kv",
    )(mem, w_bf)


def _in_proj_kernel(x_ref, wall_ref, gq_ref, wuq_ref, gkv_ref, wuk_ref, wuvt_ref,
                    qa_ref, qb_ref, ka_ref, kb_ref, lng_ref, lnb_ref, ws_ref, bs_ref,
                    km_ref, vm_ref, bgate_ref, wbg_ref, wbm_ref,
                    q_out, k_out, vt_out, part_out, g0_out):
    tm = x_ref.shape[0]
    xb = x_ref[...].astype(BF)

    z1 = _dot(xb, wall_ref[:, C_Q0:C_UV0])
    cqn = _rms_norm(z1[:, C_Q0:C_KV0], gq_ref[...]).astype(BF)
    ckvn = _rms_norm(z1[:, C_KV0:C_KR0], gkv_ref[...]).astype(BF)
    k_rope = _rope(z1[:, C_KR0:C_UV0], ka_ref[...], kb_ref[...])
    q_all = _dot(cqn, wuq_ref[...])
    k_all = _dot(ckvn, wuk_ref[...])
    qa, qb = qa_ref[...], qb_ref[...]
    for h in range(N_HEADS):
        sl = slice(h * HEAD_PAD, (h + 1) * HEAD_PAD)
        q_out[:, sl] = _rope(q_all[:, sl], qa, qb).astype(BF)
        k_out[:, sl] = (k_all[:, sl] + k_rope).astype(BF)
    vt_out[0] = _dot_nt(wuvt_ref[...], ckvn).astype(BF)

    zuv = _dot(xb, wall_ref[:, C_UV0:C_QM0])
    u = jax.nn.gelu(zuv[:, :GM_W])
    vb = _layer_norm(jax.nn.gelu(zuv[:, GM_W:]), lng_ref[...], lnb_ref[...]).astype(BF)
    nch = tm // GM_CHUNK
    sp_cols = []
    for g in range(GM_GROUPS):
        gsl = slice(g * GM_CHUNK, (g + 1) * GM_CHUNK)
        rhs = jnp.concatenate([vb[c * GM_CHUNK:(c + 1) * GM_CHUNK, gsl] for c in range(nch)], axis=1)
        spg = _dot(ws_ref[g], rhs)
        bias = bs_ref[g]
        sp_cols.append(jnp.concatenate(
            [spg[:, c * GM_CHUNK:(c + 1) * GM_CHUNK] + bias for c in range(nch)], axis=0))
    sp = jnp.concatenate(sp_cols, axis=1)
    pb = _dot((u * sp).astype(BF), wbg_ref[...])

    qm = _dot(xb, wall_ref[:, C_QM0:C_G0]).astype(BF)
    km, vm = km_ref[0], vm_ref[0]
    oc_cols = []
    for h in range(MEM_HEADS):
        hsl = slice(h * MEM_HEAD_DIM, (h + 1) * MEM_HEAD_DIM)
        s = _dot_nt(qm[:, hsl], km[:, hsl])
        e = jnp.exp2(s - jnp.max(s, axis=-1, keepdims=True))
        o = _dot(e.astype(BF), vm[:, hsl])
        oc_cols.append(o / jnp.sum(e, axis=-1, keepdims=True))
    pc = _dot(jnp.concatenate(oc_cols, axis=1).astype(BF), wbm_ref[...])

    def gate(j):
        zg = _dot(xb, wall_ref[:, C_G0 + j * D_MODEL:C_G0 + (j + 1) * D_MODEL])
        return jax.nn.sigmoid(zg + bgate_ref[j:j + 1, :])

    part = gate(1) * pb
    part = part + gate(2) * pc
    part_out[...] = part
    g0_out[...] = gate(0)


def _in_proj(x2d, km, vm, w, tabs, *, seq, tm):
    n = x2d.shape[0]
    tiles_per_seq = seq // tm
    row = lambda i: (i, 0)
    tab = lambda i: (i % tiles_per_seq, 0)
    memb = lambda i: (i // tiles_per_seq, 0, 0)
    m = km.shape[1]
    in_specs = [
        pl.BlockSpec((tm, D_MODEL), row),
        _const_spec((D_MODEL, C_END)),
        _const_spec((1, Q_LORA)),
        _const_spec((Q_LORA, N_HEADS * HEAD_PAD)),
        _const_spec((1, KV_LORA)),
        _const_spec((KV_LORA, N_HEADS * HEAD_PAD)),
        _const_spec((N_HEADS * V_HEAD, KV_LORA)),
        pl.BlockSpec((tm, HEAD_PAD), tab),
        pl.BlockSpec((tm, HEAD_PAD), tab),
        pl.BlockSpec((tm, HEAD_PAD), tab),
        pl.BlockSpec((tm, HEAD_PAD), tab),
        _const_spec((1, GM_W)),
        _const_spec((1, GM_W)),
        _const_spec((GM_GROUPS, GM_CHUNK, GM_CHUNK)),
        _const_spec((GM_GROUPS, GM_CHUNK, GM_CHUNK)),
        pl.BlockSpec((1, m, MEM_W), memb),
        pl.BlockSpec((1, m, MEM_W), memb),
        _const_spec((3, D_MODEL)),
        _const_spec((GM_W, D_MODEL)),
        _const_spec((MEM_W, D_MODEL)),
    ]
    out_specs = [
        pl.BlockSpec((tm, N_HEADS * HEAD_PAD), row),
        pl.BlockSpec((tm, N_HEADS * HEAD_PAD), row),
        pl.BlockSpec((1, N_HEADS * V_HEAD, tm), lambda i: (i, 0, 0)),
        pl.BlockSpec((tm, D_MODEL), row),
        pl.BlockSpec((tm, D_MODEL), row),
    ]
    out_shape = [
        jax.ShapeDtypeStruct((n, N_HEADS * HEAD_PAD), BF),
        jax.ShapeDtypeStruct((n, N_HEADS * HEAD_PAD), BF),
        jax.ShapeDtypeStruct((n // tm, N_HEADS * V_HEAD, tm), BF),
        jax.ShapeDtypeStruct((n, D_MODEL), F32),
        jax.ShapeDtypeStruct((n, D_MODEL), F32),
    ]
    return pl.pallas_call(
        _in_proj_kernel,
        grid=(n // tm,),
        in_specs=in_specs,
        out_specs=out_specs,
        out_shape=out_shape,
        compiler_params=_params(1),
        name="in_proj",
    )(x2d, w["w_all"], w["g_q"], w["w_uq"], w["g_kv"], w["w_uk"], w["w_uvt"],
      tabs["qa"], tabs["qb"], tabs["ka"], tabs["kb"], w["gm_ln_g"], w["gm_ln_b"],
      w["w_s"], w["b_s"], km, vm, w["b_gate"], w["w_br_gmlp"], w["w_br_mem"])


def _mla_attn_kernel(q_ref, k_ref, vt_ref, o_ref, *, nkc, tkc):
    q = q_ref[0]
    tq = q.shape[0]
    ones = jnp.ones((ONES_ROWS, tkc), BF)

    def body(c, carry):
        m, acc = carry
        start = pl.multiple_of(c * tkc, tkc)
        s = _dot_nt(k_ref[0, pl.ds(start, tkc), :], q)
        m_new = jnp.maximum(m, jnp.max(s, axis=0, keepdims=True))
        p = jnp.exp2(s - m_new).astype(BF)
        v_ext = jnp.concatenate([vt_ref[0, c], ones], axis=0)
        acc = acc * jnp.exp2(m - m_new) + _dot(v_ext, p)
        return m_new, acc

    m0 = jnp.full((1, tq), -jnp.inf, F32)
    acc0 = jnp.zeros((V_HEAD + ONES_ROWS, tq), F32)
    _, acc = lax.fori_loop(0, nkc, body, (m0, acc0))
    o_ref[0] = (acc[:V_HEAD] / acc[V_HEAD:V_HEAD + 1]).astype(BF)


def _mla_attn(q, k, vt, *, tq):
    b, s, _ = q.shape
    nkc, tkc = vt.shape[1], vt.shape[3]
    return pl.pallas_call(
        functools.partial(_mla_attn_kernel, nkc=nkc, tkc=tkc),
        grid=(b, N_HEADS, s // tq),
        in_specs=[pl.BlockSpec((1, tq, HEAD_PAD), lambda bi, h, qi: (bi, qi, h)),
                  pl.BlockSpec((1, s, HEAD_PAD), lambda bi, h, qi: (bi, 0, h)),
                  pl.BlockSpec((1, nkc, V_HEAD, tkc), lambda bi, h, qi: (bi, 0, h, 0))],
        out_specs=pl.BlockSpec((1, V_HEAD, tq), lambda bi, h, qi: (bi, h, qi)),
        out_shape=jax.ShapeDtypeStruct((b, N_HEADS * V_HEAD, s), BF),
        compiler_params=_params(3),
        name="mla_attn",
    )(q, k, vt)


def _merge_kernel(ot_ref, g0_ref, part_ref, x_ref, wba_ref, wo_ref, g_ref, b_ref, y_ref):
    pa = _dot_tn(ot_ref[0], wba_ref[...])
    merged = g0_ref[...] * pa + part_ref[...]
    r = ALPHA * x_ref[...] + _dot(merged.astype(BF), wo_ref[...])
    y_ref[...] = _layer_norm(r, g_ref[...], b_ref[...])


def _merge(ot, g0, part, x2d, w, *, seq, tm):
    n = x2d.shape[0]
    tiles_per_seq = seq // tm
    row = lambda i: (i, 0)
    return pl.pallas_call(
        _merge_kernel,
        grid=(n // tm,),
        in_specs=[pl.BlockSpec((1, N_HEADS * V_HEAD, tm),
                               lambda i: (i // tiles_per_seq, 0, i % tiles_per_seq)),
                  pl.BlockSpec((tm, D_MODEL), row),
                  pl.BlockSpec((tm, D_MODEL), row),
                  pl.BlockSpec((tm, D_MODEL), row),
                  _const_spec((N_HEADS * V_HEAD, D_MODEL)),
                  _const_spec((D_MODEL, D_MODEL)),
                  _const_spec((1, D_MODEL)),
                  _const_spec((1, D_MODEL))],
        out_specs=pl.BlockSpec((tm, D_MODEL), row),
        out_shape=jax.ShapeDtypeStruct((n, D_MODEL), F32),
        compiler_params=_params(1),
        name="merge",
    )(ot, g0, part, x2d, w["w_br_mla"], w["w_o"], w["ln1_g"], w["ln1_b"])


FF_CHUNK = 256
HALO = 8


def _conv_ffn_kernel(y_ref, prev_ref, next_ref, win_ref, cw_ref, cb_ref, wout_ref, g_ref, b_ref,
                     o_ref, act_ref, *, tiles_per_seq):
    tm = y_ref.shape[0]
    i = pl.program_id(0)
    keep_prev = (i % tiles_per_seq != 0).astype(F32)
    keep_next = (i % tiles_per_seq != tiles_per_seq - 1).astype(F32)
    y = y_ref[...]
    yb = jnp.concatenate([prev_ref[...] * keep_prev, y, next_ref[...] * keep_next], axis=0).astype(BF)
    rows = tm + 2 * HALO

    def conv(hcol, c0):
        csl = slice(c0, c0 + FF_CHUNK)
        up = pltpu.roll(hcol, 1, 0)[HALO:HALO + tm]
        dn = pltpu.roll(hcol, rows - 1, 0)[HALO:HALO + tm]
        return (cb_ref[:, csl] + up * cw_ref[0:1, csl] + hcol[HALO:HALO + tm] * cw_ref[1:2, csl]
                + dn * cw_ref[2:3, csl])

    for j in range(D_FF // FF_CHUNK):
        a0, g0 = j * FF_CHUNK, D_FF + j * FF_CHUNK
        ca = conv(_dot(yb, win_ref[:, a0:a0 + FF_CHUNK]), a0)
        cg = conv(_dot(yb, win_ref[:, g0:g0 + FF_CHUNK]), g0)
        act_ref[:, a0:a0 + FF_CHUNK] = (jax.nn.silu(ca) * cg).astype(BF)
    r = ALPHA * y + _dot(act_ref[...], wout_ref[...])
    o_ref[...] = _layer_norm(r, g_ref[...], b_ref[...])


def _conv_ffn(y2d, w, *, seq, tm):
    n = y2d.shape[0]
    tiles_per_seq = seq // tm
    nb = tm // HALO
    last = n // HALO - 1
    row = lambda i: (i, 0)
    return pl.pallas_call(
        functools.partial(_conv_ffn_kernel, tiles_per_seq=tiles_per_seq),
        grid=(n // tm,),
        in_specs=[pl.BlockSpec((tm, D_MODEL), row),
                  pl.BlockSpec((HALO, D_MODEL), lambda i: (jnp.maximum(i * nb - 1, 0), 0)),
                  pl.BlockSpec((HALO, D_MODEL), lambda i: (jnp.minimum((i + 1) * nb, last), 0)),
                  _const_spec((D_MODEL, 2 * D_FF)),
                  _const_spec((3, 2 * D_FF)),
                  _const_spec((1, 2 * D_FF)),
                  _const_spec((D_FF, D_MODEL)),
                  _const_spec((1, D_MODEL)),
                  _const_spec((1, D_MODEL))],
        out_specs=pl.BlockSpec((tm, D_MODEL), row),
        out_shape=jax.ShapeDtypeStruct((n, D_MODEL), F32),
        scratch_shapes=[pltpu.VMEM((tm, D_FF), BF)],
        compiler_params=_params(1),
        name="conv_ffn",
    )(y2d, y2d, y2d, w["w_ffn_in"], w["conv_w"], w["conv_b"], w["w_ffn_out"], w["ln2_g"], w["ln2_b"])


def _rope_tables(seq):
    pos = jnp.arange(seq, dtype=F32)
    inv = ROPE_THETA ** (-jnp.arange(0, QK_ROPE, 2, dtype=F32) / QK_ROPE)
    ang = pos[:, None] * inv[None, :]
    cos2 = jnp.tile(jnp.cos(ang), (1, 2))
    sin2 = jnp.tile(jnp.sin(ang), (1, 2))
    c = (QK_NOPE + QK_ROPE) ** -0.5 * LOG2E
    z_nope = jnp.zeros((seq, QK_NOPE), F32)
    z_tail = jnp.zeros((seq, HEAD_PAD - QK_NOPE - QK_ROPE), F32)
    return {
        "qa": jnp.concatenate([jnp.full((seq, QK_NOPE), c, F32), c * cos2, z_tail], axis=1),
        "qb": jnp.concatenate([z_nope, c * sin2, z_tail], axis=1),
        "ka": jnp.concatenate([z_nope, cos2, z_tail], axis=1),
        "kb": jnp.concatenate([z_nope, sin2, z_tail], axis=1),
    }


def _rot_cols(w_rope):
    half = QK_ROPE // 2
    return jnp.concatenate([-w_rope[..., half:], w_rope[..., :half]], axis=-1)


def _layer_weights(l, p):
    w_in = p["w_in"][l]
    offs = (0, Q_LORA, Q_LORA + KV_LORA, Q_LORA + KV_LORA + QK_ROPE)
    w_q, w_kv, w_kr = (w_in[:, offs[0]:offs[1]], w_in[:, offs[1]:offs[2]], w_in[:, offs[2]:offs[3]])
    rest = w_in[:, offs[3]:]
    w_kr_pad = jnp.concatenate([jnp.zeros((D_MODEL, ROPE_LANE), F32), w_kr, _rot_cols(w_kr)], axis=1)
    w_all = jnp.concatenate([w_q, w_kv, w_kr_pad, rest], axis=1).astype(BF)

    w_uq = p["w_uq"][l].reshape(Q_LORA, N_HEADS, QK_NOPE + QK_ROPE)
    w_uq = jnp.concatenate([w_uq, _rot_cols(w_uq[..., QK_NOPE:])], axis=-1)
    w_ukv = p["w_ukv"][l].reshape(KV_LORA, N_HEADS, QK_NOPE + V_HEAD)
    w_uk = jnp.concatenate([w_ukv[..., :QK_NOPE], jnp.zeros((KV_LORA, N_HEADS, HEAD_PAD - QK_NOPE), F32)],
                           axis=-1)
    w_uvt = w_ukv[..., QK_NOPE:].reshape(KV_LORA, N_HEADS * V_HEAD).T
    b_s = jnp.broadcast_to(p["b_s"][l][:, :, None], (GM_GROUPS, GM_CHUNK, GM_CHUNK))
    return {
        "w_all": w_all,
        "g_q": p["g_q"][l][None, :],
        "w_uq": w_uq.reshape(Q_LORA, N_HEADS * HEAD_PAD).astype(BF),
        "g_kv": p["g_kv"][l][None, :],
        "w_uk": w_uk.reshape(KV_LORA, N_HEADS * HEAD_PAD).astype(BF),
        "w_uvt": w_uvt.astype(BF),
        "gm_ln_g": p["gm_ln_g"][l][None, :],
        "gm_ln_b": p["gm_ln_b"][l][None, :],
        "w_s": p["w_s"][l].astype(BF),
        "b_s": b_s,
        "w_mem_kv": p["w_mem_kv"][l].astype(BF),
        "b_gate": p["b_gate"][l].reshape(3, D_MODEL),
        "w_br_mla": p["w_br_mla"][l].astype(BF),
        "w_br_gmlp": p["w_br_gmlp"][l].astype(BF),
        "w_br_mem": p["w_br_mem"][l].astype(BF),
        "w_o": p["w_o"][l].astype(BF),
        "ln1_g": p["ln1_g"][l][None, :],
        "ln1_b": p["ln1_b"][l][None, :],
        "w_ffn_in": p["w_ffn_in"][l].astype(BF),
        "conv_w": p["conv_w"][l],
        "conv_b": p["conv_b"][l][None, :],
        "w_ffn_out": p["w_ffn_out"][l].astype(BF),
        "ln2_g": p["ln2_g"][l][None, :],
        "ln2_b": p["ln2_b"][l][None, :],
    }


def _tiles(seq):
    pick = lambda pref: max(t for t in (128, 256, 512) if t <= pref and seq % t == 0)
    return pick(256), pick(512), pick(512), pick(512)


def _trunk(x, mem, weights):
    b, seq, _ = x.shape
    tm_a, tq, tm_c, tm_f = _tiles(seq)
    tabs = _rope_tables(seq)
    x2d = x.reshape(b * seq, D_MODEL)
    for w in weights:
        km, vm = _mem_kv(mem, w["w_mem_kv"])
        q, k, vt, part, g0 = _in_proj(x2d, km, vm, w, tabs, seq=seq, tm=tm_a)
        ot = _mla_attn(q.reshape(b, seq, -1), k.reshape(b, seq, -1),
                       vt.reshape(b, seq // tm_a, N_HEADS * V_HEAD, tm_a), tq=tq)
        y = _merge(ot, g0, part, x2d, w, seq=seq, tm=tm_c)
        x2d = _conv_ffn(y, w, seq=seq, tm=tm_f)
    return x2d.reshape(b, seq, D_MODEL)


def kernel(x_prompt, x_sample, mem_prompt, mem_sample, w_in, g_q, w_uq, g_kv, w_ukv, gm_ln_g, gm_ln_b,
           w_s, b_s, w_mem_kv, b_gate, w_br_mla, w_br_gmlp, w_br_mem, w_o, ln1_g, ln1_b, w_ffn_in,
           conv_w, conv_b, w_ffn_out, ln2_g, ln2_b):
    p = dict(w_in=w_in, g_q=g_q, w_uq=w_uq, g_kv=g_kv, w_ukv=w_ukv, gm_ln_g=gm_ln_g, gm_ln_b=gm_ln_b,
             w_s=w_s, b_s=b_s, w_mem_kv=w_mem_kv, b_gate=b_gate, w_br_mla=w_br_mla,
             w_br_gmlp=w_br_gmlp, w_br_mem=w_br_mem, w_o=w_o, ln1_g=ln1_g, ln1_b=ln1_b,
             w_ffn_in=w_ffn_in, conv_w=conv_w, conv_b=conv_b, w_ffn_out=w_ffn_out, ln2_g=ln2_g,
             ln2_b=ln2_b)
    weights = [_layer_weights(l, p) for l in range(DEPTH)]
    return (_trunk(x_prompt, mem_prompt, weights), _trunk(x_sample, mem_sample, weights))
```

```python
import functools
import math

import jax
import jax.numpy as jnp
from jax import lax
from jax.experimental import pallas as pl
from jax.experimental.pallas import tpu as pltpu

D_MODEL = 1024
DEPTH = 2
N_HEADS = 8
Q_LORA = 384
KV_LORA = 256
QK_NOPE = 64
QK_ROPE = 32
V_HEAD = 64
ROPE_THETA = 10000.0
GM_GROUPS = 4
GM_CHUNK = 128
GM_W = 512
MEM_HEADS = 4
MEM_HEAD_DIM = 128
MEM_W = 512
D_FF = 2816
LN_EPS = 1e-5
RMS_EPS = 1e-6
ALPHA = (2 * DEPTH) ** 0.25
LOG2E = 1.4426950408889634

HEAD_PAD = 128
ROPE_LANE = QK_NOPE
ONES_ROWS = 16

C_Q0, C_KV0, C_KR0, C_UV0, C_QM0, C_G0 = 0, 384, 640, 768, 1792, 2304
C_END = C_G0 + 3 * D_MODEL

VMEM_LIMIT = 58 * 1024 * 1024

BF = jnp.bfloat16
F32 = jnp.float32


def _dot(a, b):
    return jnp.dot(a, b, preferred_element_type=F32)


def _dot_nt(a, b):
    return lax.dot_general(a, b, (((1,), (1,)), ((), ())), preferred_element_type=F32)


def _dot_tn(a, b):
    return lax.dot_general(a, b, (((0,), (0,)), ((), ())), preferred_element_type=F32)


def _layer_norm(x, g, b):
    mu = jnp.mean(x, axis=-1, keepdims=True)
    xc = x - mu
    var = jnp.mean(xc * xc, axis=-1, keepdims=True)
    return xc * lax.rsqrt(var + LN_EPS) * g + b


def _rms_norm(x, g):
    return x * lax.rsqrt(jnp.mean(x * x, axis=-1, keepdims=True) + RMS_EPS) * g


def _rope(x, ta, tb):
    return x * ta + pltpu.roll(x, HEAD_PAD - QK_ROPE, 1) * tb


def _const_spec(shape):
    nd = len(shape)
    return pl.BlockSpec(shape, lambda *_: (0,) * nd, pipeline_mode=pl.Buffered(1))


def _params(n_axes):
    return pltpu.CompilerParams(dimension_semantics=("arbitrary",) * n_axes,
                                vmem_limit_bytes=VMEM_LIMIT)


def _mem_kv_kernel(mem_ref, w_ref, km_ref, vm_ref):
    kv = _dot(mem_ref[0].astype(BF), w_ref[...])
    km_ref[0] = (kv[:, :MEM_W] * (MEM_HEAD_DIM ** -0.5 * LOG2E)).astype(BF)
    vm_ref[0] = kv[:, MEM_W:].astype(BF)


def _mem_kv(mem, w_bf):
    b, m, _ = mem.shape
    return pl.pallas_call(
        _mem_kv_kernel,
        grid=(b,),
        in_specs=[pl.BlockSpec((1, m, D_MODEL), lambda i: (i, 0, 0)),
                  _const_spec((D_MODEL, 2 * MEM_W))],
        out_specs=[pl.BlockSpec((1, m, MEM_W), lambda i: (i, 0, 0)),
                   pl.BlockSpec((1, m, MEM_W), lambda i: (i, 0, 0))],
        out_shape=[jax.ShapeDtypeStruct((b, m, MEM_W), BF),
                   jax.ShapeDtypeStruct((b, m, MEM_W), BF)],
        compiler_params=_params(1),
        name="mem_kv",
    )(mem, w_bf)


def _in_proj_kernel(x_ref, wall_ref, gq_ref, wuqt_ref, gkv_ref, wuk_ref, wuvt_ref,
                    qa_ref, qb_ref, ka_ref, kb_ref, lng_ref, lnb_ref, ws_ref, bs_ref,
                    km_ref, vm_ref, bgate_ref, wbg_ref, wbm_ref,
                    qt_out, k_out, vt_out, part_out, g0_out):
    tm = x_ref.shape[0]
    xb = x_ref[...].astype(BF)

    z1 = _dot(xb, wall_ref[:, C_Q0:C_UV0])
    cqn = _rms_norm(z1[:, C_Q0:C_KV0], gq_ref[...]).astype(BF)
    ckvn = _rms_norm(z1[:, C_KV0:C_KR0], gkv_ref[...]).astype(BF)
    k_rope = _rope(z1[:, C_KR0:C_UV0], ka_ref[...], kb_ref[...])
    qt_all = _dot_nt(wuqt_ref[...], cqn)
    k_all = _dot(ckvn, wuk_ref[...])
    qa, qb = qa_ref[...], qb_ref[...]
    for h in range(N_HEADS):
        sl = slice(h * HEAD_PAD, (h + 1) * HEAD_PAD)
        qt = qt_all[sl, :]
        rot = jnp.concatenate([qt[QK_ROPE:], qt[:QK_ROPE]], axis=0)
        qt_out[0, sl, :] = (qt * qa + rot * qb).astype(BF)
        k_out[:, sl] = (k_all[:, sl] + k_rope).astype(BF)
    vt_out[0] = _dot_nt(wuvt_ref[...], ckvn).astype(BF)

    zuv = _dot(xb, wall_ref[:, C_UV0:C_QM0])
    u = jax.nn.gelu(zuv[:, :GM_W])
    vb = _layer_norm(jax.nn.gelu(zuv[:, GM_W:]), lng_ref[...], lnb_ref[...]).astype(BF)
    nch = tm // GM_CHUNK
    sp_cols = []
    for g in range(GM_GROUPS):
        gsl = slice(g * GM_CHUNK, (g + 1) * GM_CHUNK)
        rhs = jnp.concatenate([vb[c * GM_CHUNK:(c + 1) * GM_CHUNK, gsl] for c in range(nch)], axis=1)
        spg = _dot(ws_ref[g], rhs)
        bias = bs_ref[g]
        sp_cols.append(jnp.concatenate(
            [spg[:, c * GM_CHUNK:(c + 1) * GM_CHUNK] + bias for c in range(nch)], axis=0))
    sp = jnp.concatenate(sp_cols, axis=1)
    pb = _dot((u * sp).astype(BF), wbg_ref[...])

    qm = _dot(xb, wall_ref[:, C_QM0:C_G0]).astype(BF)
    km, vm = km_ref[0], vm_ref[0]
    oc_cols = []
    for h in range(MEM_HEADS):
        hsl = slice(h * MEM_HEAD_DIM, (h + 1) * MEM_HEAD_DIM)
        s = _dot_nt(qm[:, hsl], km[:, hsl])
        e = jnp.exp2(s - jnp.max(s, axis=-1, keepdims=True))
        o = _dot(e.astype(BF), vm[:, hsl])
        oc_cols.append(o / jnp.sum(e, axis=-1, keepdims=True))
    pc = _dot(jnp.concatenate(oc_cols, axis=1).astype(BF), wbm_ref[...])

    def gate(j):
        zg = _dot(xb, wall_ref[:, C_G0 + j * D_MODEL:C_G0 + (j + 1) * D_MODEL])
        return jax.nn.sigmoid(zg + bgate_ref[j:j + 1, :])

    part = gate(1) * pb
    part = part + gate(2) * pc
    part_out[...] = part
    g0_out[...] = gate(0)


def _in_proj(x2d, km, vm, w, tabs, *, seq, tm):
    n = x2d.shape[0]
    tiles_per_seq = seq // tm
    row = lambda i: (i, 0)
    tab = lambda i: (i % tiles_per_seq, 0)
    tab_t = lambda i: (0, i % tiles_per_seq)
    seq_t = lambda i: (i // tiles_per_seq, 0, i % tiles_per_seq)
    memb = lambda i: (i // tiles_per_seq, 0, 0)
    m = km.shape[1]
    in_specs = [
        pl.BlockSpec((tm, D_MODEL), row),
        _const_spec((D_MODEL, C_END)),
        _const_spec((1, Q_LORA)),
        _const_spec((N_HEADS * HEAD_PAD, Q_LORA)),
        _const_spec((1, KV_LORA)),
        _const_spec((KV_LORA, N_HEADS * HEAD_PAD)),
        _const_spec((N_HEADS * V_HEAD, KV_LORA)),
        pl.BlockSpec((HEAD_PAD, tm), tab_t),
        pl.BlockSpec((HEAD_PAD, tm), tab_t),
        pl.BlockSpec((tm, HEAD_PAD), tab),
        pl.BlockSpec((tm, HEAD_PAD), tab),
        _const_spec((1, GM_W)),
        _const_spec((1, GM_W)),
        _const_spec((GM_GROUPS, GM_CHUNK, GM_CHUNK)),
        _const_spec((GM_GROUPS, GM_CHUNK, GM_CHUNK)),
        pl.BlockSpec((1, m, MEM_W), memb),
        pl.BlockSpec((1, m, MEM_W), memb),
        _const_spec((3, D_MODEL)),
        _const_spec((GM_W, D_MODEL)),
        _const_spec((MEM_W, D_MODEL)),
    ]
    out_specs = [
        pl.BlockSpec((1, N_HEADS * HEAD_PAD, tm), seq_t),
        pl.BlockSpec((tm, N_HEADS * HEAD_PAD), row),
        pl.BlockSpec((1, N_HEADS * V_HEAD, tm), lambda i: (i, 0, 0)),
        pl.BlockSpec((tm, D_MODEL), row),
        pl.BlockSpec((tm, D_MODEL), row),
    ]
    out_shape = [
        jax.ShapeDtypeStruct((n // seq, N_HEADS * HEAD_PAD, seq), BF),
        jax.ShapeDtypeStruct((n, N_HEADS * HEAD_PAD), BF),
        jax.ShapeDtypeStruct((n // tm, N_HEADS * V_HEAD, tm), BF),
        jax.ShapeDtypeStruct((n, D_MODEL), F32),
        jax.ShapeDtypeStruct((n, D_MODEL), F32),
    ]
    return pl.pallas_call(
        _in_proj_kernel,
        grid=(n // tm,),
        in_specs=in_specs,
        out_specs=out_specs,
        out_shape=out_shape,
        compiler_params=_params(1),
        name="in_proj",
    )(x2d, w["w_all"], w["g_q"], w["w_uqt"], w["g_kv"], w["w_uk"], w["w_uvt"],
      tabs["qa"], tabs["qb"], tabs["ka"], tabs["kb"], w["gm_ln_g"], w["gm_ln_b"],
      w["w_s"], w["b_s"], km, vm, w["b_gate"], w["w_br_gmlp"], w["w_br_mem"])


ATTN_HEADS_PER_STEP = 4


def _mla_attn_kernel(qt_ref, k_ref, vt_ref, o_ref, *, nkc, tkc):
    tq = qt_ref.shape[2]
    hg = ATTN_HEADS_PER_STEP
    ones = jnp.ones((ONES_ROWS, tkc), BF)

    def body(c, carry):
        start = pl.multiple_of(c * tkc, tkc)
        ss = []
        for h in range(hg):
            ksl = slice(h * HEAD_PAD, (h + 1) * HEAD_PAD)
            ss.append(_dot(k_ref[0, pl.ds(start, tkc), ksl], qt_ref[0, ksl, :]))
        ps, ms = [], []
        for h in range(hg):
            m_new = jnp.maximum(carry[h][0], jnp.max(ss[h], axis=0, keepdims=True))
            ms.append(m_new)
            ps.append(jnp.exp2(ss[h] - m_new).astype(BF))
        out = []
        for h in range(hg):
            m, acc = carry[h]
            v_ext = jnp.concatenate([vt_ref[0, c, h * V_HEAD:(h + 1) * V_HEAD, :], ones], axis=0)
            out.append((ms[h], acc * jnp.exp2(m - ms[h]) + _dot(v_ext, ps[h])))
        return tuple(out)

    m0 = jnp.full((1, tq), -jnp.inf, F32)
    acc0 = jnp.zeros((V_HEAD + ONES_ROWS, tq), F32)
    res = lax.fori_loop(0, nkc, body, tuple((m0, acc0) for _ in range(hg)), unroll=2)
    for h in range(hg):
        acc = res[h][1]
        o_ref[0, h * V_HEAD:(h + 1) * V_HEAD, :] = (acc[:V_HEAD] / acc[V_HEAD:V_HEAD + 1]).astype(BF)


def _mla_attn(qt, k, vt, *, tq):
    b, s, _ = k.shape
    nkc, tkc = vt.shape[1], vt.shape[3]
    hg = ATTN_HEADS_PER_STEP
    return pl.pallas_call(
        functools.partial(_mla_attn_kernel, nkc=nkc, tkc=tkc),
        grid=(b, N_HEADS // hg, s // tq),
        in_specs=[pl.BlockSpec((1, hg * HEAD_PAD, tq), lambda bi, g, qi: (bi, g, qi)),
                  pl.BlockSpec((1, s, hg * HEAD_PAD), lambda bi, g, qi: (bi, 0, g)),
                  pl.BlockSpec((1, nkc, hg * V_HEAD, tkc), lambda bi, g, qi: (bi, 0, g, 0))],
        out_specs=pl.BlockSpec((1, hg * V_HEAD, tq), lambda bi, g, qi: (bi, g, qi)),
        out_shape=jax.ShapeDtypeStruct((b, N_HEADS * V_HEAD, s), BF),
        compiler_params=_params(3),
        name="mla_attn",
    )(qt, k, vt)


def _merge_kernel(ot_ref, g0_ref, part_ref, x_ref, wba_ref, wo_ref, g_ref, b_ref, y_ref):
    pa = _dot_tn(ot_ref[0], wba_ref[...])
    merged = g0_ref[...] * pa + part_ref[...]
    r = ALPHA * x_ref[...] + _dot(merged.astype(BF), wo_ref[...])
    y_ref[...] = _layer_norm(r, g_ref[...], b_ref[...])


def _merge(ot, g0, part, x2d, w, *, seq, tm):
    n = x2d.shape[0]
    tiles_per_seq = seq // tm
    row = lambda i: (i, 0)
    return pl.pallas_call(
        _merge_kernel,
        grid=(n // tm,),
        in_specs=[pl.BlockSpec((1, N_HEADS * V_HEAD, tm),
                               lambda i: (i // tiles_per_seq, 0, i % tiles_per_seq)),
                  pl.BlockSpec((tm, D_MODEL), row),
                  pl.BlockSpec((tm, D_MODEL), row),
                  pl.BlockSpec((tm, D_MODEL), row),
                  _const_spec((N_HEADS * V_HEAD, D_MODEL)),
                  _const_spec((D_MODEL, D_MODEL)),
                  _const_spec((1, D_MODEL)),
                  _const_spec((1, D_MODEL))],
        out_specs=pl.BlockSpec((tm, D_MODEL), row),
        out_shape=jax.ShapeDtypeStruct((n, D_MODEL), F32),
        compiler_params=_params(1),
        name="merge",
    )(ot, g0, part, x2d, w["w_br_mla"], w["w_o"], w["ln1_g"], w["ln1_b"])


FF_CHUNK = 256
HALO = 8


def _conv_ffn_kernel(y_ref, prev_ref, next_ref, win_ref, cw_ref, cb_ref, wout_ref, g_ref, b_ref,
                     o_ref, act_ref, *, tiles_per_seq):
    tm = y_ref.shape[0]
    i = pl.program_id(0)
    keep_prev = (i % tiles_per_seq != 0).astype(F32)
    keep_next = (i % tiles_per_seq != tiles_per_seq - 1).astype(F32)
    y = y_ref[...]
    yb = jnp.concatenate([prev_ref[...] * keep_prev, y, next_ref[...] * keep_next], axis=0).astype(BF)
    rows = tm + 2 * HALO

    def conv(hcol, c0):
        csl = slice(c0, c0 + FF_CHUNK)
        up = pltpu.roll(hcol, 1, 0)[HALO:HALO + tm]
        dn = pltpu.roll(hcol, rows - 1, 0)[HALO:HALO + tm]
        return (cb_ref[:, csl] + up * cw_ref[0:1, csl] + hcol[HALO:HALO + tm] * cw_ref[1:2, csl]
                + dn * cw_ref[2:3, csl])

    for j in range(D_FF // FF_CHUNK):
        a0, g0 = j * FF_CHUNK, D_FF + j * FF_CHUNK
        ca = conv(_dot(yb, win_ref[:, a0:a0 + FF_CHUNK]), a0)
        cg = conv(_dot(yb, win_ref[:, g0:g0 + FF_CHUNK]), g0)
        act_ref[:, a0:a0 + FF_CHUNK] = (jax.nn.silu(ca) * cg).astype(BF)
    r = ALPHA * y + _dot(act_ref[...], wout_ref[...])
    o_ref[...] = _layer_norm(r, g_ref[...], b_ref[...])


def _conv_ffn(y2d, w, *, seq, tm):
    n = y2d.shape[0]
    tiles_per_seq = seq // tm
    nb = tm // HALO
    last = n // HALO - 1
    row = lambda i: (i, 0)
    return pl.pallas_call(
        functools.partial(_conv_ffn_kernel, tiles_per_seq=tiles_per_seq),
        grid=(n // tm,),
        in_specs=[pl.BlockSpec((tm, D_MODEL), row),
                  pl.BlockSpec((HALO, D_MODEL), lambda i: (jnp.maximum(i * nb - 1, 0), 0)),
                  pl.BlockSpec((HALO, D_MODEL), lambda i: (jnp.minimum((i + 1) * nb, last), 0)),
                  _const_spec((D_MODEL, 2 * D_FF)),
                  _const_spec((3, 2 * D_FF)),
                  _const_spec((1, 2 * D_FF)),
                  _const_spec((D_FF, D_MODEL)),
                  _const_spec((1, D_MODEL)),
                  _const_spec((1, D_MODEL))],
        out_specs=pl.BlockSpec((tm, D_MODEL), row),
        out_shape=jax.ShapeDtypeStruct((n, D_MODEL), F32),
        scratch_shapes=[pltpu.VMEM((tm, D_FF), BF)],
        compiler_params=_params(1),
        name="conv_ffn",
    )(y2d, y2d, y2d, w["w_ffn_in"], w["conv_w"], w["conv_b"], w["w_ffn_out"], w["ln2_g"], w["ln2_b"])


def _rope_tables(seq):
    pos = jnp.arange(seq, dtype=F32)
    inv = ROPE_THETA ** (-jnp.arange(0, QK_ROPE, 2, dtype=F32) / QK_ROPE)
    ang = pos[:, None] * inv[None, :]
    cos2 = jnp.tile(jnp.cos(ang), (1, 2))
    sin2 = jnp.tile(jnp.sin(ang), (1, 2))
    c = (QK_NOPE + QK_ROPE) ** -0.5 * LOG2E
    z_nope = jnp.zeros((seq, QK_NOPE), F32)
    z_tail = jnp.zeros((seq, HEAD_PAD - QK_NOPE - QK_ROPE), F32)
    return {
        "qa": jnp.concatenate([jnp.full((seq, QK_NOPE), c, F32), c * cos2, z_tail], axis=1).T,
        "qb": jnp.concatenate([z_nope, c * sin2, z_tail], axis=1).T,
        "ka": jnp.concatenate([z_nope, cos2, z_tail], axis=1),
        "kb": jnp.concatenate([z_nope, sin2, z_tail], axis=1),
    }


def _rot_cols(w_rope):
    half = QK_ROPE // 2
    return jnp.concatenate([-w_rope[..., half:], w_rope[..., :half]], axis=-1)


def _layer_weights(l, p):
    w_in = p["w_in"][l]
    offs = (0, Q_LORA, Q_LORA + KV_LORA, Q_LORA + KV_LORA + QK_ROPE)
    w_q, w_kv, w_kr = (w_in[:, offs[0]:offs[1]], w_in[:, offs[1]:offs[2]], w_in[:, offs[2]:offs[3]])
    rest = w_in[:, offs[3]:]
    w_kr_pad = jnp.concatenate([jnp.zeros((D_MODEL, ROPE_LANE), F32), w_kr, _rot_cols(w_kr)], axis=1)
    w_all = jnp.concatenate([w_q, w_kv, w_kr_pad, rest], axis=1).astype(BF)

    w_uq = p["w_uq"][l].reshape(Q_LORA, N_HEADS, QK_NOPE + QK_ROPE)
    w_uq = jnp.concatenate([w_uq, _rot_cols(w_uq[..., QK_NOPE:])], axis=-1)
    w_ukv = p["w_ukv"][l].reshape(KV_LORA, N_HEADS, QK_NOPE + V_HEAD)
    w_uk = jnp.concatenate([w_ukv[..., :QK_NOPE], jnp.zeros((KV_LORA, N_HEADS, HEAD_PAD - QK_NOPE), F32)],
                           axis=-1)
    w_uvt = w_ukv[..., QK_NOPE:].reshape(KV_LORA, N_HEADS * V_HEAD).T
    b_s = jnp.broadcast_to(p["b_s"][l][:, :, None], (GM_GROUPS, GM_CHUNK, GM_CHUNK))
    return {
        "w_all": w_all,
        "g_q": p["g_q"][l][None, :],
        "w_uqt": w_uq.reshape(Q_LORA, N_HEADS * HEAD_PAD).T.astype(BF),
        "g_kv": p["g_kv"][l][None, :],
        "w_uk": w_uk.reshape(KV_LORA, N_HEADS * HEAD_PAD).astype(BF),
        "w_uvt": w_uvt.astype(BF),
        "gm_ln_g": p["gm_ln_g"][l][None, :],
        "gm_ln_b": p["gm_ln_b"][l][None, :],
        "w_s": p["w_s"][l].astype(BF),
        "b_s": b_s,
        "w_mem_kv": p["w_mem_kv"][l].astype(BF),
        "b_gate": p["b_gate"][l].reshape(3, D_MODEL),
        "w_br_mla": p["w_br_mla"][l].astype(BF),
        "w_br_gmlp": p["w_br_gmlp"][l].astype(BF),
        "w_br_mem": p["w_br_mem"][l].astype(BF),
        "w_o": p["w_o"][l].astype(BF),
        "ln1_g": p["ln1_g"][l][None, :],
        "ln1_b": p["ln1_b"][l][None, :],
        "w_ffn_in": p["w_ffn_in"][l].astype(BF),
        "conv_w": p["conv_w"][l],
        "conv_b": p["conv_b"][l][None, :],
        "w_ffn_out": p["w_ffn_out"][l].astype(BF),
        "ln2_g": p["ln2_g"][l][None, :],
        "ln2_b": p["ln2_b"][l][None, :],
    }


def _tiles(seq):
    pick = lambda pref: max(t for t in (128, 256, 512) if t <= pref and seq % t == 0)
    return pick(256), pick(512), pick(512), pick(512)


def _trunk(x, mem, weights):
    b, seq, _ = x.shape
    tm_a, tq, tm_c, tm_f = _tiles(seq)
    tabs = _rope_tables(seq)
    x2d = x.reshape(b * seq, D_MODEL)
    for w in weights:
        km, vm = _mem_kv(mem, w["w_mem_kv"])
        qt, k, vt, part, g0 = _in_proj(x2d, km, vm, w, tabs, seq=seq, tm=tm_a)
        ot = _mla_attn(qt, k.reshape(b, seq, -1),
                       vt.reshape(b, seq // tm_a, N_HEADS * V_HEAD, tm_a), tq=tq)
        y = _merge(ot, g0, part, x2d, w, seq=seq, tm=tm_c)
        x2d = _conv_ffn(y, w, seq=seq, tm=tm_f)
    return x2d.reshape(b, seq, D_MODEL)


def kernel(x_prompt, x_sample, mem_prompt, mem_sample, w_in, g_q, w_uq, g_kv, w_ukv, gm_ln_g, gm_ln_b,
           w_s, b_s, w_mem_kv, b_gate, w_br_mla, w_br_gmlp, w_br_mem, w_o, ln1_g, ln1_b, w_ffn_in,
           conv_w, conv_b, w_ffn_out, ln2_g, ln2_b):
    p = dict(w_in=w_in, g_q=g_q, w_uq=w_uq, g_kv=g_kv, w_ukv=w_ukv, gm_ln_g=gm_ln_g, gm_ln_b=gm_ln_b,
             w_s=w_s, b_s=b_s, w_mem_kv=w_mem_kv, b_gate=b_gate, w_br_mla=w_br_mla,
             w_br_gmlp=w_br_gmlp, w_br_mem=w_br_mem, w_o=w_o, ln1_g=ln1_g, ln1_b=ln1_b,
             w_ffn_in=w_ffn_in, conv_w=conv_w, conv_b=conv_b, w_ffn_out=w_ffn_out, ln2_g=ln2_g,
             ln2_b=ln2_b)
    weights = [_layer_weights(l, p) for l in range(DEPTH)]
    return (_trunk(x_prompt, mem_prompt, weights), _trunk(x_sample, mem_sample, weights))
```

```python
import functools
import math

import jax
import jax.numpy as jnp
from jax import lax
from jax.experimental import pallas as pl
from jax.experimental.pallas import tpu as pltpu

D_MODEL = 1024
DEPTH = 2
N_HEADS = 8
Q_LORA = 384
KV_LORA = 256
QK_NOPE = 64
QK_ROPE = 32
V_HEAD = 64
ROPE_THETA = 10000.0
GM_GROUPS = 4
GM_CHUNK = 128
GM_W = 512
MEM_HEADS = 4
MEM_HEAD_DIM = 128
MEM_W = 512
D_FF = 2816
LN_EPS = 1e-5
RMS_EPS = 1e-6
ALPHA = (2 * DEPTH) ** 0.25
LOG2E = 1.4426950408889634

HEAD_PAD = 128
ROPE_LANE = QK_NOPE
ONES_ROWS = 16

C_Q0, C_KV0, C_KR0, C_UV0, C_QM0, C_G0 = 0, 384, 640, 768, 1792, 2304
C_END = C_G0 + 3 * D_MODEL

VMEM_LIMIT = 58 * 1024 * 1024

BF = jnp.bfloat16
F32 = jnp.float32


def _dot(a, b):
    return jnp.dot(a, b, preferred_element_type=F32)


def _dot_nt(a, b):
    return lax.dot_general(a, b, (((1,), (1,)), ((), ())), preferred_element_type=F32)


def _dot_tn(a, b):
    return lax.dot_general(a, b, (((0,), (0,)), ((), ())), preferred_element_type=F32)


def _layer_norm(x, g, b):
    mu = jnp.mean(x, axis=-1, keepdims=True)
    xc = x - mu
    var = jnp.mean(xc * xc, axis=-1, keepdims=True)
    return xc * lax.rsqrt(var + LN_EPS) * g + b


def _rms_norm(x, g):
    return x * lax.rsqrt(jnp.mean(x * x, axis=-1, keepdims=True) + RMS_EPS) * g


def _rope(x, ta, tb):
    return x * ta + pltpu.roll(x, HEAD_PAD - QK_ROPE, 1) * tb


def _const_spec(shape):
    nd = len(shape)
    return pl.BlockSpec(shape, lambda *_: (0,) * nd, pipeline_mode=pl.Buffered(1))


def _params(n_axes):
    return pltpu.CompilerParams(dimension_semantics=("arbitrary",) * n_axes,
                                vmem_limit_bytes=VMEM_LIMIT)


def _mem_kv_kernel(mem_ref, w_ref, km_ref, vm_ref):
    kv = _dot(mem_ref[0].astype(BF), w_ref[...])
    km_ref[0] = (kv[:, :MEM_W] * (MEM_HEAD_DIM ** -0.5 * LOG2E)).astype(BF)
    vm_ref[0] = kv[:, MEM_W:].astype(BF)


def _mem_kv(mem, w_bf):
    b, m, _ = mem.shape
    return pl.pallas_call(
        _mem_kv_kernel,
        grid=(b,),
        in_specs=[pl.BlockSpec((1, m, D_MODEL), lambda i: (i, 0, 0)),
                  _const_spec((D_MODEL, 2 * MEM_W))],
        out_specs=[pl.BlockSpec((1, m, MEM_W), lambda i: (i, 0, 0)),
                   pl.BlockSpec((1, m, MEM_W), lambda i: (i, 0, 0))],
        out_shape=[jax.ShapeDtypeStruct((b, m, MEM_W), BF),
                   jax.ShapeDtypeStruct((b, m, MEM_W), BF)],
        compiler_params=_params(1),
        name="mem_kv",
    )(mem, w_bf)


def _in_proj_kernel(x_ref, wall_ref, gq_ref, wuqt_ref, gkv_ref, wuk_ref, wuvt_ref,
                    qa_ref, qb_ref, ka_ref, kb_ref, lng_ref, lnb_ref, ws_ref, bs_ref,
                    km_ref, vm_ref, bgate_ref, wbg_ref, wbm_ref,
                    qt_out, k_out, vt_out, part_out, g0_out):
    tm = x_ref.shape[0]
    xb = x_ref[...].astype(BF)

    z1 = _dot(xb, wall_ref[:, C_Q0:C_UV0])
    cqn = _rms_norm(z1[:, C_Q0:C_KV0], gq_ref[...]).astype(BF)
    ckvn = _rms_norm(z1[:, C_KV0:C_KR0], gkv_ref[...]).astype(BF)
    k_rope = _rope(z1[:, C_KR0:C_UV0], ka_ref[...], kb_ref[...])
    qt_all = _dot_nt(wuqt_ref[...], cqn)
    k_all = _dot(ckvn, wuk_ref[...])
    qa, qb = qa_ref[...], qb_ref[...]
    for h in range(N_HEADS):
        sl = slice(h * HEAD_PAD, (h + 1) * HEAD_PAD)
        qt = qt_all[sl, :]
        rot = jnp.concatenate([qt[QK_ROPE:], qt[:QK_ROPE]], axis=0)
        qt_out[0, sl, :] = (qt * qa + rot * qb).astype(BF)
        k_out[:, sl] = (k_all[:, sl] + k_rope).astype(BF)
    vt_out[0] = _dot_nt(wuvt_ref[...], ckvn).astype(BF)

    zuv = _dot(xb, wall_ref[:, C_UV0:C_QM0])
    u = jax.nn.gelu(zuv[:, :GM_W])
    vb = _layer_norm(jax.nn.gelu(zuv[:, GM_W:]), lng_ref[...], lnb_ref[...]).astype(BF)
    nch = tm // GM_CHUNK
    sp_cols = []
    for g in range(GM_GROUPS):
        gsl = slice(g * GM_CHUNK, (g + 1) * GM_CHUNK)
        rhs = jnp.concatenate([vb[c * GM_CHUNK:(c + 1) * GM_CHUNK, gsl] for c in range(nch)], axis=1)
        spg = _dot(ws_ref[g], rhs)
        bias = bs_ref[g]
        sp_cols.append(jnp.concatenate(
            [spg[:, c * GM_CHUNK:(c + 1) * GM_CHUNK] + bias for c in range(nch)], axis=0))
    sp = jnp.concatenate(sp_cols, axis=1)
    pb = _dot((u * sp).astype(BF), wbg_ref[...])

    qm = _dot(xb, wall_ref[:, C_QM0:C_G0]).astype(BF)
    km, vm = km_ref[0], vm_ref[0]
    oc_cols = []
    for h in range(MEM_HEADS):
        hsl = slice(h * MEM_HEAD_DIM, (h + 1) * MEM_HEAD_DIM)
        s = _dot_nt(qm[:, hsl], km[:, hsl])
        e = jnp.exp2(s - jnp.max(s, axis=-1, keepdims=True))
        o = _dot(e.astype(BF), vm[:, hsl])
        oc_cols.append(o / jnp.sum(e, axis=-1, keepdims=True))
    pc = _dot(jnp.concatenate(oc_cols, axis=1).astype(BF), wbm_ref[...])

    def gate(j):
        zg = _dot(xb, wall_ref[:, C_G0 + j * D_MODEL:C_G0 + (j + 1) * D_MODEL])
        return jax.nn.sigmoid(zg + bgate_ref[j:j + 1, :])

    part = gate(1) * pb
    part = part + gate(2) * pc
    part_out[...] = part.astype(BF)
    g0_out[...] = gate(0).astype(BF)


def _in_proj(x2d, km, vm, w, tabs, *, seq, tm):
    n = x2d.shape[0]
    tiles_per_seq = seq // tm
    row = lambda i: (i, 0)
    tab = lambda i: (i % tiles_per_seq, 0)
    tab_t = lambda i: (0, i % tiles_per_seq)
    seq_t = lambda i: (i // tiles_per_seq, 0, i % tiles_per_seq)
    memb = lambda i: (i // tiles_per_seq, 0, 0)
    m = km.shape[1]
    in_specs = [
        pl.BlockSpec((tm, D_MODEL), row),
        _const_spec((D_MODEL, C_END)),
        _const_spec((1, Q_LORA)),
        _const_spec((N_HEADS * HEAD_PAD, Q_LORA)),
        _const_spec((1, KV_LORA)),
        _const_spec((KV_LORA, N_HEADS * HEAD_PAD)),
        _const_spec((N_HEADS * V_HEAD, KV_LORA)),
        pl.BlockSpec((HEAD_PAD, tm), tab_t),
        pl.BlockSpec((HEAD_PAD, tm), tab_t),
        pl.BlockSpec((tm, HEAD_PAD), tab),
        pl.BlockSpec((tm, HEAD_PAD), tab),
        _const_spec((1, GM_W)),
        _const_spec((1, GM_W)),
        _const_spec((GM_GROUPS, GM_CHUNK, GM_CHUNK)),
        _const_spec((GM_GROUPS, GM_CHUNK, GM_CHUNK)),
        pl.BlockSpec((1, m, MEM_W), memb),
        pl.BlockSpec((1, m, MEM_W), memb),
        _const_spec((3, D_MODEL)),
        _const_spec((GM_W, D_MODEL)),
        _const_spec((MEM_W, D_MODEL)),
    ]
    out_specs = [
        pl.BlockSpec((1, N_HEADS * HEAD_PAD, tm), seq_t),
        pl.BlockSpec((tm, N_HEADS * HEAD_PAD), row),
        pl.BlockSpec((1, N_HEADS * V_HEAD, tm), lambda i: (i, 0, 0)),
        pl.BlockSpec((tm, D_MODEL), row),
        pl.BlockSpec((tm, D_MODEL), row),
    ]
    out_shape = [
        jax.ShapeDtypeStruct((n // seq, N_HEADS * HEAD_PAD, seq), BF),
        jax.ShapeDtypeStruct((n, N_HEADS * HEAD_PAD), BF),
        jax.ShapeDtypeStruct((n // tm, N_HEADS * V_HEAD, tm), BF),
        jax.ShapeDtypeStruct((n, D_MODEL), BF),
        jax.ShapeDtypeStruct((n, D_MODEL), BF),
    ]
    return pl.pallas_call(
        _in_proj_kernel,
        grid=(n // tm,),
        in_specs=in_specs,
        out_specs=out_specs,
        out_shape=out_shape,
        compiler_params=_params(1),
        name="in_proj",
    )(x2d, w["w_all"], w["g_q"], w["w_uqt"], w["g_kv"], w["w_uk"], w["w_uvt"],
      tabs["qa"], tabs["qb"], tabs["ka"], tabs["kb"], w["gm_ln_g"], w["gm_ln_b"],
      w["w_s"], w["b_s"], km, vm, w["b_gate"], w["w_br_gmlp"], w["w_br_mem"])


ATTN_HEADS_PER_STEP = 2


def _mla_attn_kernel(qt_ref, k_ref, vt_ref, o_ref, s_scr, *, nkc, tkc):
    tq = qt_ref.shape[2]
    hg = ATTN_HEADS_PER_STEP
    ones = jnp.ones((ONES_ROWS, tkc), BF)

    def scores(c, slot, ms):
        start = pl.multiple_of(c * tkc, tkc)
        out = []
        for h in range(hg):
            ksl = slice(h * HEAD_PAD, (h + 1) * HEAD_PAD)
            s = _dot(k_ref[0, pl.ds(start, tkc), ksl], qt_ref[0, ksl, :])
            s_scr[slot, h] = s
            out.append(jnp.maximum(ms[h], jnp.max(s, axis=0, keepdims=True)))
        return out

    def accumulate(c, slot, m_old, m_new, accs):
        ps = [jnp.exp2(s_scr[slot, h] - m_new[h]).astype(BF) for h in range(hg)]
        out = []
        for h in range(hg):
            v_ext = jnp.concatenate([vt_ref[0, c, h * V_HEAD:(h + 1) * V_HEAD, :], ones], axis=0)
            out.append(accs[h] * jnp.exp2(m_old[h] - m_new[h]) + _dot(v_ext, ps[h]))
        return out

    m_init = [jnp.full((1, tq), -jnp.inf, F32) for _ in range(hg)]
    acc_init = [jnp.zeros((V_HEAD + ONES_ROWS, tq), F32) for _ in range(hg)]
    m_first = scores(0, 0, m_init)

    def body(j, carry):
        m_a_old, m_a, accs = (list(t) for t in carry)
        a = 2 * j
        m_b = scores(a + 1, 1, m_a)
        accs = accumulate(a, 0, m_a_old, m_a, accs)
        m_next = scores(a + 2, 0, m_b)
        accs = accumulate(a + 1, 1, m_a, m_b, accs)
        return tuple(m_b), tuple(m_next), tuple(accs)

    m_a_old, m_a, accs = lax.fori_loop(0, nkc // 2 - 1, body,
                                       (tuple(m_init), tuple(m_first), tuple(acc_init)))
    a = nkc - 2
    m_b = scores(a + 1, 1, list(m_a))
    accs = accumulate(a, 0, list(m_a_old), list(m_a), list(accs))
    accs = accumulate(a + 1, 1, list(m_a), m_b, accs)
    for h in range(hg):
        acc = accs[h]
        o_ref[0, h * V_HEAD:(h + 1) * V_HEAD, :] = (acc[:V_HEAD] / acc[V_HEAD:V_HEAD + 1]).astype(BF)


def _mla_attn(qt, k, vt, *, tq):
    b, s, _ = k.shape
    nkc, tkc = vt.shape[1], vt.shape[3]
    hg = ATTN_HEADS_PER_STEP
    return pl.pallas_call(
        functools.partial(_mla_attn_kernel, nkc=nkc, tkc=tkc),
        grid=(b, N_HEADS // hg, s // tq),
        in_specs=[pl.BlockSpec((1, hg * HEAD_PAD, tq), lambda bi, g, qi: (bi, g, qi)),
                  pl.BlockSpec((1, s, hg * HEAD_PAD), lambda bi, g, qi: (bi, 0, g)),
                  pl.BlockSpec((1, nkc, hg * V_HEAD, tkc), lambda bi, g, qi: (bi, 0, g, 0))],
        out_specs=pl.BlockSpec((1, hg * V_HEAD, tq), lambda bi, g, qi: (bi, g, qi)),
        out_shape=jax.ShapeDtypeStruct((b, N_HEADS * V_HEAD, s), BF),
        scratch_shapes=[pltpu.VMEM((2, hg, tkc, tq), F32)],
        compiler_params=_params(3),
        name="mla_attn",
    )(qt, k, vt)


def _merge_kernel(ot_ref, g0_ref, part_ref, x_ref, wba_ref, wo_ref, g_ref, b_ref, y_ref):
    pa = _dot_tn(ot_ref[0], wba_ref[...])
    merged = g0_ref[...].astype(F32) * pa + part_ref[...].astype(F32)
    r = ALPHA * x_ref[...] + _dot(merged.astype(BF), wo_ref[...])
    y_ref[...] = _layer_norm(r, g_ref[...], b_ref[...])


def _merge(ot, g0, part, x2d, w, *, seq, tm):
    n = x2d.shape[0]
    tiles_per_seq = seq // tm
    row = lambda i: (i, 0)
    return pl.pallas_call(
        _merge_kernel,
        grid=(n // tm,),
        in_specs=[pl.BlockSpec((1, N_HEADS * V_HEAD, tm),
                               lambda i: (i // tiles_per_seq, 0, i % tiles_per_seq)),
                  pl.BlockSpec((tm, D_MODEL), row),
                  pl.BlockSpec((tm, D_MODEL), row),
                  pl.BlockSpec((tm, D_MODEL), row),
                  _const_spec((N_HEADS * V_HEAD, D_MODEL)),
                  _const_spec((D_MODEL, D_MODEL)),
                  _const_spec((1, D_MODEL)),
                  _const_spec((1, D_MODEL))],
        out_specs=pl.BlockSpec((tm, D_MODEL), row),
        out_shape=jax.ShapeDtypeStruct((n, D_MODEL), F32),
        compiler_params=_params(1),
        name="merge",
    )(ot, g0, part, x2d, w["w_br_mla"], w["w_o"], w["ln1_g"], w["ln1_b"])


FF_CHUNK = 256
HALO = 8


def _conv_ffn_kernel(y_ref, prev_ref, next_ref, win_ref, cw_ref, cb_ref, wout_ref, g_ref, b_ref,
                     o_ref, act_ref, *, tiles_per_seq):
    tm = y_ref.shape[0]
    i = pl.program_id(0)
    keep_prev = (i % tiles_per_seq != 0).astype(F32)
    keep_next = (i % tiles_per_seq != tiles_per_seq - 1).astype(F32)
    y = y_ref[...]
    yb = jnp.concatenate([prev_ref[...] * keep_prev, y, next_ref[...] * keep_next], axis=0).astype(BF)
    rows = tm + 2 * HALO

    def conv(hcol, c0):
        csl = slice(c0, c0 + FF_CHUNK)
        up = pltpu.roll(hcol, 1, 0)[HALO:HALO + tm]
        dn = pltpu.roll(hcol, rows - 1, 0)[HALO:HALO + tm]
        return (cb_ref[:, csl] + up * cw_ref[0:1, csl] + hcol[HALO:HALO + tm] * cw_ref[1:2, csl]
                + dn * cw_ref[2:3, csl])

    for j in range(D_FF // FF_CHUNK):
        a0, g0 = j * FF_CHUNK, D_FF + j * FF_CHUNK
        ca = conv(_dot(yb, win_ref[:, a0:a0 + FF_CHUNK]), a0)
        cg = conv(_dot(yb, win_ref[:, g0:g0 + FF_CHUNK]), g0)
        act_ref[:, a0:a0 + FF_CHUNK] = (jax.nn.silu(ca) * cg).astype(BF)
    r = ALPHA * y + _dot(act_ref[...], wout_ref[...])
    o_ref[...] = _layer_norm(r, g_ref[...], b_ref[...])


def _conv_ffn(y2d, w, *, seq, tm):
    n = y2d.shape[0]
    tiles_per_seq = seq // tm
    nb = tm // HALO
    last = n // HALO - 1
    row = lambda i: (i, 0)
    return pl.pallas_call(
        functools.partial(_conv_ffn_kernel, tiles_per_seq=tiles_per_seq),
        grid=(n // tm,),
        in_specs=[pl.BlockSpec((tm, D_MODEL), row),
                  pl.BlockSpec((HALO, D_MODEL), lambda i: (jnp.maximum(i * nb - 1, 0), 0)),
                  pl.BlockSpec((HALO, D_MODEL), lambda i: (jnp.minimum((i + 1) * nb, last), 0)),
                  _const_spec((D_MODEL, 2 * D_FF)),
                  _const_spec((3, 2 * D_FF)),
                  _const_spec((1, 2 * D_FF)),
                  _const_spec((D_FF, D_MODEL)),
                  _const_spec((1, D_MODEL)),
                  _const_spec((1, D_MODEL))],
        out_specs=pl.BlockSpec((tm, D_MODEL), row),
        out_shape=jax.ShapeDtypeStruct((n, D_MODEL), F32),
        scratch_shapes=[pltpu.VMEM((tm, D_FF), BF)],
        compiler_params=_params(1),
        name="conv_ffn",
    )(y2d, y2d, y2d, w["w_ffn_in"], w["conv_w"], w["conv_b"], w["w_ffn_out"], w["ln2_g"], w["ln2_b"])


def _rope_tables(seq):
    pos = jnp.arange(seq, dtype=F32)
    inv = ROPE_THETA ** (-jnp.arange(0, QK_ROPE, 2, dtype=F32) / QK_ROPE)
    ang = pos[:, None] * inv[None, :]
    cos2 = jnp.tile(jnp.cos(ang), (1, 2))
    sin2 = jnp.tile(jnp.sin(ang), (1, 2))
    c = (QK_NOPE + QK_ROPE) ** -0.5 * LOG2E
    z_nope = jnp.zeros((seq, QK_NOPE), F32)
    z_tail = jnp.zeros((seq, HEAD_PAD - QK_NOPE - QK_ROPE), F32)
    return {
        "qa": jnp.concatenate([jnp.full((seq, QK_NOPE), c, F32), c * cos2, z_tail], axis=1).T,
        "qb": jnp.concatenate([z_nope, c * sin2, z_tail], axis=1).T,
        "ka": jnp.concatenate([z_nope, cos2, z_tail], axis=1),
        "kb": jnp.concatenate([z_nope, sin2, z_tail], axis=1),
    }


def _rot_cols(w_rope):
    half = QK_ROPE // 2
    return jnp.concatenate([-w_rope[..., half:], w_rope[..., :half]], axis=-1)


def _layer_weights(l, p):
    w_in = p["w_in"][l]
    offs = (0, Q_LORA, Q_LORA + KV_LORA, Q_LORA + KV_LORA + QK_ROPE)
    w_q, w_kv, w_kr = (w_in[:, offs[0]:offs[1]], w_in[:, offs[1]:offs[2]], w_in[:, offs[2]:offs[3]])
    rest = w_in[:, offs[3]:]
    w_kr_pad = jnp.concatenate([jnp.zeros((D_MODEL, ROPE_LANE), F32), w_kr, _rot_cols(w_kr)], axis=1)
    w_all = jnp.concatenate([w_q, w_kv, w_kr_pad, rest], axis=1).astype(BF)

    w_uq = p["w_uq"][l].reshape(Q_LORA, N_HEADS, QK_NOPE + QK_ROPE)
    w_uq = jnp.concatenate([w_uq, _rot_cols(w_uq[..., QK_NOPE:])], axis=-1)
    w_ukv = p["w_ukv"][l].reshape(KV_LORA, N_HEADS, QK_NOPE + V_HEAD)
    w_uk = jnp.concatenate([w_ukv[..., :QK_NOPE], jnp.zeros((KV_LORA, N_HEADS, HEAD_PAD - QK_NOPE), F32)],
                           axis=-1)
    w_uvt = w_ukv[..., QK_NOPE:].reshape(KV_LORA, N_HEADS * V_HEAD).T
    b_s = jnp.broadcast_to(p["b_s"][l][:, :, None], (GM_GROUPS, GM_CHUNK, GM_CHUNK))
    return {
        "w_all": w_all,
        "g_q": p["g_q"][l][None, :],
        "w_uqt": w_uq.reshape(Q_LORA, N_HEADS * HEAD_PAD).T.astype(BF),
        "g_kv": p["g_kv"][l][None, :],
        "w_uk": w_uk.reshape(KV_LORA, N_HEADS * HEAD_PAD).astype(BF),
        "w_uvt": w_uvt.astype(BF),
        "gm_ln_g": p["gm_ln_g"][l][None, :],
        "gm_ln_b": p["gm_ln_b"][l][None, :],
        "w_s": p["w_s"][l].astype(BF),
        "b_s": b_s,
        "w_mem_kv": p["w_mem_kv"][l].astype(BF),
        "b_gate": p["b_gate"][l].reshape(3, D_MODEL),
        "w_br_mla": p["w_br_mla"][l].astype(BF),
        "w_br_gmlp": p["w_br_gmlp"][l].astype(BF),
        "w_br_mem": p["w_br_mem"][l].astype(BF),
        "w_o": p["w_o"][l].astype(BF),
        "ln1_g": p["ln1_g"][l][None, :],
        "ln1_b": p["ln1_b"][l][None, :],
        "w_ffn_in": p["w_ffn_in"][l].astype(BF),
        "conv_w": p["conv_w"][l],
        "conv_b": p["conv_b"][l][None, :],
        "w_ffn_out": p["w_ffn_out"][l].astype(BF),
        "ln2_g": p["ln2_g"][l][None, :],
        "ln2_b": p["ln2_b"][l][None, :],
    }


def _tiles(seq):
    pick = lambda pref: max(t for t in (128, 256, 512) if t <= pref and seq % t == 0)
    return pick(512), pick(512), pick(512), pick(512)


def _trunk(x, mem, weights):
    b, seq, _ = x.shape
    tm_a, tq, tm_c, tm_f = _tiles(seq)
    tabs = _rope_tables(seq)
    x2d = x.reshape(b * seq, D_MODEL)
    for w in weights:
        km, vm = _mem_kv(mem, w["w_mem_kv"])
        qt, k, vt, part, g0 = _in_proj(x2d, km, vm, w, tabs, seq=seq, tm=tm_a)
        ot = _mla_attn(qt, k.reshape(b, seq, -1),
                       vt.reshape(b, seq // tm_a, N_HEADS * V_HEAD, tm_a), tq=tq)
        y = _merge(ot, g0, part, x2d, w, seq=seq, tm=tm_c)
        x2d = _conv_ffn(y, w, seq=seq, tm=tm_f)
    return x2d.reshape(b, seq, D_MODEL)


def kernel(x_prompt, x_sample, mem_prompt, mem_sample, w_in, g_q, w_uq, g_kv, w_ukv, gm_ln_g, gm_ln_b,
           w_s, b_s, w_mem_kv, b_gate, w_br_mla, w_br_gmlp, w_br_mem, w_o, ln1_g, ln1_b, w_ffn_in,
           conv_w, conv_b, w_ffn_out, ln2_g, ln2_b):
    p = dict(w_in=w_in, g_q=g_q, w_uq=w_uq, g_kv=g_kv, w_ukv=w_ukv, gm_ln_g=gm_ln_g, gm_ln_b=gm_ln_b,
             w_s=w_s, b_s=b_s, w_mem_kv=w_mem_kv, b_gate=b_gate, w_br_mla=w_br_mla,
             w_br_gmlp=w_br_gmlp, w_br_mem=w_br_mem, w_o=w_o, ln1_g=ln1_g, ln1_b=ln1_b,
             w_ffn_in=w_ffn_in, conv_w=conv_w, conv_b=conv_b, w_ffn_out=w_ffn_out, ln2_g=ln2_g,
             ln2_b=ln2_b)
    weights = [_layer_weights(l, p) for l in range(DEPTH)]
    return (_trunk(x_prompt, mem_prompt, weights), _trunk(x_sample, mem_sample, weights))
```

```python
import functools

import jax
import jax.numpy as jnp
from jax import lax
from jax.experimental import pallas as pl
from jax.experimental.pallas import tpu as pltpu

D_MODEL = 1024
DEPTH = 2
N_HEADS = 8
Q_LORA = 384
KV_LORA = 256
QK_NOPE = 64
QK_ROPE = 32
V_HEAD = 64
ROPE_THETA = 10000.0
GM_GROUPS = 4
GM_CHUNK = 128
GM_W = 512
MEM_HEADS = 4
MEM_HEAD_DIM = 128
MEM_W = 512
D_FF = 2816
LN_EPS = 1e-5
RMS_EPS = 1e-6
ALPHA = (2 * DEPTH) ** 0.25
LOG2E = 1.4426950408889634

HEAD_PAD = 128
ROPE_LANE = QK_NOPE

C_Q0, C_KV0, C_KR0, C_UV0, C_QM0, C_G0 = 0, 384, 640, 768, 1792, 2304
C_END = C_G0 + 3 * D_MODEL

VMEM_LIMIT = 58 * 1024 * 1024

BF = jnp.bfloat16
F32 = jnp.float32


def _dot(a, b):
    return jnp.dot(a, b, preferred_element_type=F32)


def _dot_nt(a, b):
    return lax.dot_general(a, b, (((1,), (1,)), ((), ())), preferred_element_type=F32)


def _dot_tn(a, b):
    return lax.dot_general(a, b, (((0,), (0,)), ((), ())), preferred_element_type=F32)


def _layer_norm(x, g, b):
    mu = jnp.mean(x, axis=-1, keepdims=True)
    xc = x - mu
    var = jnp.mean(xc * xc, axis=-1, keepdims=True)
    return xc * lax.rsqrt(var + LN_EPS) * g + b


def _rms_norm(x, g):
    return x * lax.rsqrt(jnp.mean(x * x, axis=-1, keepdims=True) + RMS_EPS) * g


def _rope(x, ta, tb):
    return x * ta + pltpu.roll(x, HEAD_PAD - QK_ROPE, 1) * tb


def _const_spec(shape):
    nd = len(shape)
    return pl.BlockSpec(shape, lambda *_: (0,) * nd, pipeline_mode=pl.Buffered(1))


def _params(n_axes):
    return pltpu.CompilerParams(dimension_semantics=("arbitrary",) * n_axes,
                                vmem_limit_bytes=VMEM_LIMIT)


def _mem_kv_kernel(mem_ref, w_ref, km_ref, vm_ref):
    kv = _dot(mem_ref[0].astype(BF), w_ref[...])
    km_ref[0] = (kv[:, :MEM_W] * (MEM_HEAD_DIM ** -0.5 * LOG2E)).astype(BF)
    vm_ref[0] = kv[:, MEM_W:].astype(BF)


def _mem_kv(mem, w_bf):
    b, m, _ = mem.shape
    return pl.pallas_call(
        _mem_kv_kernel,
        grid=(b,),
        in_specs=[pl.BlockSpec((1, m, D_MODEL), lambda i: (i, 0, 0)),
                  _const_spec((D_MODEL, 2 * MEM_W))],
        out_specs=[pl.BlockSpec((1, m, MEM_W), lambda i: (i, 0, 0)),
                   pl.BlockSpec((1, m, MEM_W), lambda i: (i, 0, 0))],
        out_shape=[jax.ShapeDtypeStruct((b, m, MEM_W), BF),
                   jax.ShapeDtypeStruct((b, m, MEM_W), BF)],
        compiler_params=_params(1),
        name="mem_kv",
    )(mem, w_bf)


def _in_proj_kernel(x_ref, wall_ref, gq_ref, wuqt_ref, gkv_ref, wuk_ref, wuvt_ref,
                    qa_ref, qb_ref, ka_ref, kb_ref, lng_ref, lnb_ref, ws_ref, bs_ref,
                    km_ref, vm_ref, bgate_ref, wbg_ref, wbm_ref,
                    qt_out, k_out, vt_out, part_out, g0_out):
    tm = x_ref.shape[0]
    xb = x_ref[...].astype(BF)

    zuv = _dot(xb, wall_ref[:, C_UV0:C_QM0])
    u = jax.nn.gelu(zuv[:, :GM_W])
    vb = _layer_norm(jax.nn.gelu(zuv[:, GM_W:]), lng_ref[...], lnb_ref[...]).astype(BF)
    nch = tm // GM_CHUNK
    sp_cols = []
    for g in range(GM_GROUPS):
        gsl = slice(g * GM_CHUNK, (g + 1) * GM_CHUNK)
        rhs = jnp.concatenate([vb[c * GM_CHUNK:(c + 1) * GM_CHUNK, gsl] for c in range(nch)], axis=1)
        spg = _dot(ws_ref[g], rhs)
        bias = bs_ref[g]
        sp_cols.append(jnp.concatenate(
            [spg[:, c * GM_CHUNK:(c + 1) * GM_CHUNK] + bias for c in range(nch)], axis=0))
    sp = jnp.concatenate(sp_cols, axis=1)
    pb = _dot((u * sp).astype(BF), wbg_ref[...])

    qm = _dot(xb, wall_ref[:, C_QM0:C_G0]).astype(BF)
    km, vm = km_ref[0], vm_ref[0]
    oc_cols = []
    for h in range(MEM_HEADS):
        hsl = slice(h * MEM_HEAD_DIM, (h + 1) * MEM_HEAD_DIM)
        s = _dot_nt(qm[:, hsl], km[:, hsl])
        e = jnp.exp2(s - jnp.max(s, axis=-1, keepdims=True))
        o = _dot(e.astype(BF), vm[:, hsl])
        oc_cols.append(o / jnp.sum(e, axis=-1, keepdims=True))
    pc = _dot(jnp.concatenate(oc_cols, axis=1).astype(BF), wbm_ref[...])

    def gate(j):
        zg = _dot(xb, wall_ref[:, C_G0 + j * D_MODEL:C_G0 + (j + 1) * D_MODEL])
        return jax.nn.sigmoid(zg + bgate_ref[j:j + 1, :])

    part = gate(1) * pb
    part = part + gate(2) * pc
    part_out[...] = part.astype(BF)
    g0_out[...] = gate(0).astype(BF)

    z1 = _dot(xb, wall_ref[:, C_Q0:C_UV0])
    cqn = _rms_norm(z1[:, C_Q0:C_KV0], gq_ref[...]).astype(BF)
    ckvn = _rms_norm(z1[:, C_KV0:C_KR0], gkv_ref[...]).astype(BF)
    k_rope = _rope(z1[:, C_KR0:C_UV0], ka_ref[...], kb_ref[...])
    qt_all = _dot_nt(wuqt_ref[...], cqn)
    k_all = _dot(ckvn, wuk_ref[...])
    qa, qb = qa_ref[...], qb_ref[...]
    for h in range(N_HEADS):
        sl = slice(h * HEAD_PAD, (h + 1) * HEAD_PAD)
        qt = qt_all[sl, :]
        rot = jnp.concatenate([qt[QK_ROPE:], qt[:QK_ROPE]], axis=0)
        qt_out[0, sl, :] = (qt * qa + rot * qb).astype(BF)
        k_out[:, sl] = (k_all[:, sl] + k_rope).astype(BF)
    vt_out[0] = _dot_nt(wuvt_ref[...], ckvn).astype(BF)


def _in_proj(x2d, km, vm, w, tabs, *, seq, tm):
    n = x2d.shape[0]
    tiles_per_seq = seq // tm
    row = lambda i: (i, 0)
    tab = lambda i: (i % tiles_per_seq, 0)
    tab_t = lambda i: (0, i % tiles_per_seq)
    seq_t = lambda i: (i // tiles_per_seq, 0, i % tiles_per_seq)
    memb = lambda i: (i // tiles_per_seq, 0, 0)
    m = km.shape[1]
    in_specs = [
        pl.BlockSpec((tm, D_MODEL), row),
        _const_spec((D_MODEL, C_END)),
        _const_spec((1, Q_LORA)),
        _const_spec((N_HEADS * HEAD_PAD, Q_LORA)),
        _const_spec((1, KV_LORA)),
        _const_spec((KV_LORA, N_HEADS * HEAD_PAD)),
        _const_spec((N_HEADS * V_HEAD, KV_LORA)),
        pl.BlockSpec((HEAD_PAD, tm), tab_t),
        pl.BlockSpec((HEAD_PAD, tm), tab_t),
        pl.BlockSpec((tm, HEAD_PAD), tab),
        pl.BlockSpec((tm, HEAD_PAD), tab),
        _const_spec((1, GM_W)),
        _const_spec((1, GM_W)),
        _const_spec((GM_GROUPS, GM_CHUNK, GM_CHUNK)),
        _const_spec((GM_GROUPS, GM_CHUNK, GM_CHUNK)),
        pl.BlockSpec((1, m, MEM_W), memb),
        pl.BlockSpec((1, m, MEM_W), memb),
        _const_spec((3, D_MODEL)),
        _const_spec((GM_W, D_MODEL)),
        _const_spec((MEM_W, D_MODEL)),
    ]
    out_specs = [
        pl.BlockSpec((1, N_HEADS * HEAD_PAD, tm), seq_t),
        pl.BlockSpec((tm, N_HEADS * HEAD_PAD), row),
        pl.BlockSpec((1, N_HEADS * V_HEAD, tm), lambda i: (i, 0, 0)),
        pl.BlockSpec((tm, D_MODEL), row),
        pl.BlockSpec((tm, D_MODEL), row),
    ]
    out_shape = [
        jax.ShapeDtypeStruct((n // seq, N_HEADS * HEAD_PAD, seq), BF),
        jax.ShapeDtypeStruct((n, N_HEADS * HEAD_PAD), BF),
        jax.ShapeDtypeStruct((n // tm, N_HEADS * V_HEAD, tm), BF),
        jax.ShapeDtypeStruct((n, D_MODEL), BF),
        jax.ShapeDtypeStruct((n, D_MODEL), BF),
    ]
    return pl.pallas_call(
        _in_proj_kernel,
        grid=(n // tm,),
        in_specs=in_specs,
        out_specs=out_specs,
        out_shape=out_shape,
        compiler_params=_params(1),
        name="in_proj",
    )(x2d, w["w_all"], w["g_q"], w["w_uqt"], w["g_kv"], w["w_uk"], w["w_uvt"],
      tabs["qa"], tabs["qb"], tabs["ka"], tabs["kb"], w["gm_ln_g"], w["gm_ln_b"],
      w["w_s"], w["b_s"], km, vm, w["b_gate"], w["w_br_gmlp"], w["w_br_mem"])


ATTN_HEADS_PER_STEP = 2
ATTN_KEY_CHUNK = 1024


def _mla_attn_kernel(qt_ref, k_ref, vt_ref, o_ref, s_scr, *, nkc, tkc, tq):
    hg = ATTN_HEADS_PER_STEP
    n_sub = qt_ref.shape[2] // tq
    vt_per_chunk = tkc // vt_ref.shape[3]

    def scores(qs, c, slot, ms):
        start = pl.multiple_of(c * tkc, tkc)
        out = []
        for h in range(hg):
            ksl = slice(h * HEAD_PAD, (h + 1) * HEAD_PAD)
            s = _dot(k_ref[0, pl.ds(start, tkc), ksl], qt_ref[0, ksl, qs])
            s_scr[slot, h] = s
            out.append(jnp.maximum(ms[h], jnp.max(s, axis=0, keepdims=True)))
        return out

    def accumulate(c, slot, m_old, m_new, carry):
        es = [jnp.exp2(s_scr[slot, h] - m_new[h]) for h in range(hg)]
        out = []
        for h in range(hg):
            acc, l = carry[h]
            vsl = slice(h * V_HEAD, (h + 1) * V_HEAD)
            vt = jnp.concatenate([vt_ref[0, c * vt_per_chunk + i, vsl, :] for i in range(vt_per_chunk)],
                                 axis=1)
            alpha = jnp.exp2(m_old[h] - m_new[h])
            out.append((acc * alpha + _dot(vt, es[h].astype(BF)),
                        l * alpha + jnp.sum(es[h], axis=0, keepdims=True)))
        return out

    m_init = [jnp.full((1, tq), -jnp.inf, F32) for _ in range(hg)]
    carry_init = [(jnp.zeros((V_HEAD, tq), F32), jnp.zeros((1, tq), F32)) for _ in range(hg)]
    q_slices = [slice(i * tq, (i + 1) * tq) for i in range(n_sub)]
    m_first = scores(q_slices[0], 0, 0, m_init)
    for i, qs in enumerate(q_slices):

        def body(j, carry, qs=qs):
            m_a_old, m_a, accs = (list(t) for t in carry)
            a = 2 * j
            m_b = scores(qs, a + 1, 1, m_a)
            accs = accumulate(a, 0, m_a_old, m_a, accs)
            m_next = scores(qs, a + 2, 0, m_b)
            accs = accumulate(a + 1, 1, m_a, m_b, accs)
            return tuple(m_b), tuple(m_next), tuple(accs)

        m_a_old, m_a, accs = lax.fori_loop(0, nkc // 2 - 1, body,
                                           (tuple(m_init), tuple(m_first), tuple(carry_init)))
        a = nkc - 2
        m_b = scores(qs, a + 1, 1, list(m_a))
        accs = accumulate(a, 0, list(m_a_old), list(m_a), list(accs))
        if i + 1 < n_sub:
            m_first = scores(q_slices[i + 1], 0, 0, m_init)
        accs = accumulate(a + 1, 1, list(m_a), m_b, accs)
        for h in range(hg):
            acc, l = accs[h]
            o_ref[0, h * V_HEAD:(h + 1) * V_HEAD, qs] = (acc / l).astype(BF)


def _mla_attn(qt, k, vt, *, tq, n_sub):
    b, s, _ = k.shape
    n_vt, t_vt = vt.shape[1], vt.shape[3]
    tkc = ATTN_KEY_CHUNK if s % (2 * ATTN_KEY_CHUNK) == 0 else t_vt
    nkc = s // tkc
    hg = ATTN_HEADS_PER_STEP
    tq_step = tq * n_sub
    return pl.pallas_call(
        functools.partial(_mla_attn_kernel, nkc=nkc, tkc=tkc, tq=tq),
        grid=(b, N_HEADS // hg, s // tq_step),
        in_specs=[pl.BlockSpec((1, hg * HEAD_PAD, tq_step), lambda bi, g, qi: (bi, g, qi)),
                  pl.BlockSpec((1, s, hg * HEAD_PAD), lambda bi, g, qi: (bi, 0, g)),
                  pl.BlockSpec((1, n_vt, hg * V_HEAD, t_vt), lambda bi, g, qi: (bi, 0, g, 0))],
        out_specs=pl.BlockSpec((1, hg * V_HEAD, tq_step), lambda bi, g, qi: (bi, g, qi)),
        out_shape=jax.ShapeDtypeStruct((b, N_HEADS * V_HEAD, s), BF),
        scratch_shapes=[pltpu.VMEM((2, hg, tkc, tq), F32)],
        compiler_params=_params(3),
        name="mla_attn",
    )(qt, k, vt)


def _merge_kernel(ot_ref, g0_ref, part_ref, x_ref, wba_ref, wo_ref, g_ref, b_ref, y_ref):
    pa = _dot_tn(ot_ref[0], wba_ref[...])
    merged = g0_ref[...].astype(F32) * pa + part_ref[...].astype(F32)
    r = ALPHA * x_ref[...] + _dot(merged.astype(BF), wo_ref[...])
    y_ref[...] = _layer_norm(r, g_ref[...], b_ref[...])


def _merge(ot, g0, part, x2d, w, *, seq, tm):
    n = x2d.shape[0]
    tiles_per_seq = seq // tm
    row = lambda i: (i, 0)
    return pl.pallas_call(
        _merge_kernel,
        grid=(n // tm,),
        in_specs=[pl.BlockSpec((1, N_HEADS * V_HEAD, tm),
                               lambda i: (i // tiles_per_seq, 0, i % tiles_per_seq)),
                  pl.BlockSpec((tm, D_MODEL), row),
                  pl.BlockSpec((tm, D_MODEL), row),
                  pl.BlockSpec((tm, D_MODEL), row),
                  _const_spec((N_HEADS * V_HEAD, D_MODEL)),
                  _const_spec((D_MODEL, D_MODEL)),
                  _const_spec((1, D_MODEL)),
                  _const_spec((1, D_MODEL))],
        out_specs=pl.BlockSpec((tm, D_MODEL), row),
        out_shape=jax.ShapeDtypeStruct((n, D_MODEL), F32),
        compiler_params=_params(1),
        name="merge",
    )(ot, g0, part, x2d, w["w_br_mla"], w["w_o"], w["ln1_g"], w["ln1_b"])


FF_CHUNK = 256
HALO = 8


def _conv_ffn_kernel(y_ref, prev_ref, next_ref, win_ref, cw_ref, cb_ref, wout_ref, g_ref, b_ref,
                     o_ref, act_ref, *, tiles_per_seq):
    tm = y_ref.shape[0]
    i = pl.program_id(0)
    keep_prev = (i % tiles_per_seq != 0).astype(F32)
    keep_next = (i % tiles_per_seq != tiles_per_seq - 1).astype(F32)
    y = y_ref[...]
    yb = jnp.concatenate([prev_ref[...] * keep_prev, y, next_ref[...] * keep_next], axis=0).astype(BF)
    rows = tm + 2 * HALO

    def conv(hcol, c0):
        csl = slice(c0, c0 + FF_CHUNK)
        up = pltpu.roll(hcol, 1, 0)[HALO:HALO + tm]
        dn = pltpu.roll(hcol, rows - 1, 0)[HALO:HALO + tm]
        return (cb_ref[:, csl] + up * cw_ref[0:1, csl] + hcol[HALO:HALO + tm] * cw_ref[1:2, csl]
                + dn * cw_ref[2:3, csl])

    for j in range(D_FF // FF_CHUNK):
        a0, g0 = j * FF_CHUNK, D_FF + j * FF_CHUNK
        ca = conv(_dot(yb, win_ref[:, a0:a0 + FF_CHUNK]), a0)
        cg = conv(_dot(yb, win_ref[:, g0:g0 + FF_CHUNK]), g0)
        act_ref[:, a0:a0 + FF_CHUNK] = (jax.nn.silu(ca) * cg).astype(BF)
    half = tm // 2
    for r0 in (0, half):
        r = ALPHA * y[r0:r0 + half] + _dot(act_ref[r0:r0 + half, :], wout_ref[...])
        o_ref[r0:r0 + half, :] = _layer_norm(r, g_ref[...], b_ref[...])


def _conv_ffn(y2d, w, *, seq, tm):
    n = y2d.shape[0]
    tiles_per_seq = seq // tm
    nb = tm // HALO
    last = n // HALO - 1
    row = lambda i: (i, 0)
    return pl.pallas_call(
        functools.partial(_conv_ffn_kernel, tiles_per_seq=tiles_per_seq),
        grid=(n // tm,),
        in_specs=[pl.BlockSpec((tm, D_MODEL), row),
                  pl.BlockSpec((HALO, D_MODEL), lambda i: (jnp.maximum(i * nb - 1, 0), 0)),
                  pl.BlockSpec((HALO, D_MODEL), lambda i: (jnp.minimum((i + 1) * nb, last), 0)),
                  _const_spec((D_MODEL, 2 * D_FF)),
                  _const_spec((3, 2 * D_FF)),
                  _const_spec((1, 2 * D_FF)),
                  _const_spec((D_FF, D_MODEL)),
                  _const_spec((1, D_MODEL)),
                  _const_spec((1, D_MODEL))],
        out_specs=pl.BlockSpec((tm, D_MODEL), row),
        out_shape=jax.ShapeDtypeStruct((n, D_MODEL), F32),
        scratch_shapes=[pltpu.VMEM((tm, D_FF), BF)],
        compiler_params=_params(1),
        name="conv_ffn",
    )(y2d, y2d, y2d, w["w_ffn_in"], w["conv_w"], w["conv_b"], w["w_ffn_out"], w["ln2_g"], w["ln2_b"])


def _rope_tables(seq):
    pos = jnp.arange(seq, dtype=F32)
    inv = ROPE_THETA ** (-jnp.arange(0, QK_ROPE, 2, dtype=F32) / QK_ROPE)
    ang = pos[:, None] * inv[None, :]
    cos2 = jnp.tile(jnp.cos(ang), (1, 2))
    sin2 = jnp.tile(jnp.sin(ang), (1, 2))
    c = (QK_NOPE + QK_ROPE) ** -0.5 * LOG2E
    z_nope = jnp.zeros((seq, QK_NOPE), F32)
    z_tail = jnp.zeros((seq, HEAD_PAD - QK_NOPE - QK_ROPE), F32)
    return {
        "qa": jnp.concatenate([jnp.full((seq, QK_NOPE), c, F32), c * cos2, z_tail], axis=1).T,
        "qb": jnp.concatenate([z_nope, c * sin2, z_tail], axis=1).T,
        "ka": jnp.concatenate([z_nope, cos2, z_tail], axis=1),
        "kb": jnp.concatenate([z_nope, sin2, z_tail], axis=1),
    }


def _rot_cols(w_rope):
    half = QK_ROPE // 2
    return jnp.concatenate([-w_rope[..., half:], w_rope[..., :half]], axis=-1)


def _layer_weights(l, p):
    w_in = p["w_in"][l]
    offs = (0, Q_LORA, Q_LORA + KV_LORA, Q_LORA + KV_LORA + QK_ROPE)
    w_q, w_kv, w_kr = (w_in[:, offs[0]:offs[1]], w_in[:, offs[1]:offs[2]], w_in[:, offs[2]:offs[3]])
    rest = w_in[:, offs[3]:]
    w_kr_pad = jnp.concatenate([jnp.zeros((D_MODEL, ROPE_LANE), F32), w_kr, _rot_cols(w_kr)], axis=1)
    w_all = jnp.concatenate([w_q, w_kv, w_kr_pad, rest], axis=1).astype(BF)

    w_uq = p["w_uq"][l].reshape(Q_LORA, N_HEADS, QK_NOPE + QK_ROPE)
    w_uq = jnp.concatenate([w_uq, _rot_cols(w_uq[..., QK_NOPE:])], axis=-1)
    w_ukv = p["w_ukv"][l].reshape(KV_LORA, N_HEADS, QK_NOPE + V_HEAD)
    w_uk = jnp.concatenate([w_ukv[..., :QK_NOPE], jnp.zeros((KV_LORA, N_HEADS, HEAD_PAD - QK_NOPE), F32)],
                           axis=-1)
    w_uvt = w_ukv[..., QK_NOPE:].reshape(KV_LORA, N_HEADS * V_HEAD).T
    b_s = jnp.broadcast_to(p["b_s"][l][:, :, None], (GM_GROUPS, GM_CHUNK, GM_CHUNK))
    return {
        "w_all": w_all,
        "g_q": p["g_q"][l][None, :],
        "w_uqt": w_uq.reshape(Q_LORA, N_HEADS * HEAD_PAD).T.astype(BF),
        "g_kv": p["g_kv"][l][None, :],
        "w_uk": w_uk.reshape(KV_LORA, N_HEADS * HEAD_PAD).astype(BF),
        "w_uvt": w_uvt.astype(BF),
        "gm_ln_g": p["gm_ln_g"][l][None, :],
        "gm_ln_b": p["gm_ln_b"][l][None, :],
        "w_s": p["w_s"][l].astype(BF),
        "b_s": b_s,
        "w_mem_kv": p["w_mem_kv"][l].astype(BF),
        "b_gate": p["b_gate"][l].reshape(3, D_MODEL),
        "w_br_mla": p["w_br_mla"][l].astype(BF),
        "w_br_gmlp": p["w_br_gmlp"][l].astype(BF),
        "w_br_mem": p["w_br_mem"][l].astype(BF),
        "w_o": p["w_o"][l].astype(BF),
        "ln1_g": p["ln1_g"][l][None, :],
        "ln1_b": p["ln1_b"][l][None, :],
        "w_ffn_in": p["w_ffn_in"][l].astype(BF),
        "conv_w": p["conv_w"][l],
        "conv_b": p["conv_b"][l][None, :],
        "w_ffn_out": p["w_ffn_out"][l].astype(BF),
        "ln2_g": p["ln2_g"][l][None, :],
        "ln2_b": p["ln2_b"][l][None, :],
    }


def _tiles(seq):
    pick = lambda pref: max(t for t in (128, 256, 512) if t <= pref and seq % t == 0)
    tq = pick(512)
    n_sub = max(n for n in (1, 2, 4) if seq % (tq * n) == 0)
    return pick(512), tq, n_sub, pick(512), pick(512)


def _trunk(x, mem, weights):
    b, seq, _ = x.shape
    tm_a, tq, n_sub, tm_c, tm_f = _tiles(seq)
    tabs = _rope_tables(seq)
    x2d = x.reshape(b * seq, D_MODEL)
    for w in weights:
        km, vm = _mem_kv(mem, w["w_mem_kv"])
        qt, k, vt, part, g0 = _in_proj(x2d, km, vm, w, tabs, seq=seq, tm=tm_a)
        ot = _mla_attn(qt, k.reshape(b, seq, -1),
                       vt.reshape(b, seq // tm_a, N_HEADS * V_HEAD, tm_a), tq=tq, n_sub=n_sub)
        y = _merge(ot, g0, part, x2d, w, seq=seq, tm=tm_c)
        x2d = _conv_ffn(y, w, seq=seq, tm=tm_f)
    return x2d.reshape(b, seq, D_MODEL)


def kernel(x_prompt, x_sample, mem_prompt, mem_sample, w_in, g_q, w_uq, g_kv, w_ukv, gm_ln_g, gm_ln_b,
           w_s, b_s, w_mem_kv, b_gate, w_br_mla, w_br_gmlp, w_br_mem, w_o, ln1_g, ln1_b, w_ffn_in,
           conv_w, conv_b, w_ffn_out, ln2_g, ln2_b):
    p = dict(w_in=w_in, g_q=g_q, w_uq=w_uq, g_kv=g_kv, w_ukv=w_ukv, gm_ln_g=gm_ln_g, gm_ln_b=gm_ln_b,
             w_s=w_s, b_s=b_s, w_mem_kv=w_mem_kv, b_gate=b_gate, w_br_mla=w_br_mla,
             w_br_gmlp=w_br_gmlp, w_br_mem=w_br_mem, w_o=w_o, ln1_g=ln1_g, ln1_b=ln1_b,
             w_ffn_in=w_ffn_in, conv_w=conv_w, conv_b=conv_b, w_ffn_out=w_ffn_out, ln2_g=ln2_g,
             ln2_b=ln2_b)
    weights = [_layer_weights(l, p) for l in range(DEPTH)]
    return (_trunk(x_prompt, mem_prompt, weights), _trunk(x_sample, mem_sample, weights))
```

```python
import functools

import jax
import jax.numpy as jnp
from jax import lax
from jax.experimental import pallas as pl
from jax.experimental.pallas import tpu as pltpu

D_MODEL = 1024
DEPTH = 2
N_HEADS = 8
Q_LORA = 384
KV_LORA = 256
QK_NOPE = 64
QK_ROPE = 32
V_HEAD = 64
ROPE_THETA = 10000.0
GM_GROUPS = 4
GM_CHUNK = 128
GM_W = 512
MEM_HEADS = 4
MEM_HEAD_DIM = 128
MEM_W = 512
D_FF = 2816
LN_EPS = 1e-5
RMS_EPS = 1e-6
ALPHA = (2 * DEPTH) ** 0.25
LOG2E = 1.4426950408889634

HEAD_PAD = 128
ROPE_LANE = QK_NOPE

C_Q0, C_KV0, C_KR0, C_UV0, C_QM0, C_G0 = 0, 384, 640, 768, 1792, 2304
C_END = C_G0 + 3 * D_MODEL

VMEM_LIMIT = 58 * 1024 * 1024

BF = jnp.bfloat16
F32 = jnp.float32


def _dot(a, b):
    return jnp.dot(a, b, preferred_element_type=F32)


def _dot_nt(a, b):
    return lax.dot_general(a, b, (((1,), (1,)), ((), ())), preferred_element_type=F32)


def _dot_tn(a, b):
    return lax.dot_general(a, b, (((0,), (0,)), ((), ())), preferred_element_type=F32)


def _layer_norm(x, g, b):
    mu = jnp.mean(x, axis=-1, keepdims=True)
    xc = x - mu
    var = jnp.mean(xc * xc, axis=-1, keepdims=True)
    return xc * lax.rsqrt(var + LN_EPS) * g + b


def _rms_norm(x, g):
    return x * lax.rsqrt(jnp.mean(x * x, axis=-1, keepdims=True) + RMS_EPS) * g


def _rope(x, ta, tb):
    return x * ta + pltpu.roll(x, HEAD_PAD - QK_ROPE, 1) * tb


def _const_spec(shape):
    nd = len(shape)
    return pl.BlockSpec(shape, lambda *_: (0,) * nd, pipeline_mode=pl.Buffered(1))


def _params(n_axes):
    return pltpu.CompilerParams(dimension_semantics=("arbitrary",) * n_axes,
                                vmem_limit_bytes=VMEM_LIMIT)


def _mem_kv_kernel(mem_ref, w_ref, km_ref, vm_ref):
    kv = _dot(mem_ref[0].astype(BF), w_ref[...])
    km_ref[0] = (kv[:, :MEM_W] * (MEM_HEAD_DIM ** -0.5 * LOG2E)).astype(BF)
    vm_ref[0] = kv[:, MEM_W:].astype(BF)


def _mem_kv(mem, w_bf):
    b, m, _ = mem.shape
    return pl.pallas_call(
        _mem_kv_kernel,
        grid=(b,),
        in_specs=[pl.BlockSpec((1, m, D_MODEL), lambda i: (i, 0, 0)),
                  _const_spec((D_MODEL, 2 * MEM_W))],
        out_specs=[pl.BlockSpec((1, m, MEM_W), lambda i: (i, 0, 0)),
                   pl.BlockSpec((1, m, MEM_W), lambda i: (i, 0, 0))],
        out_shape=[jax.ShapeDtypeStruct((b, m, MEM_W), BF),
                   jax.ShapeDtypeStruct((b, m, MEM_W), BF)],
        compiler_params=_params(1),
        name="mem_kv",
    )(mem, w_bf)


def _in_proj_kernel(x_ref, wall_ref, gq_ref, wuqt_ref, gkv_ref, wuk_ref, wuvt_ref,
                    qa_ref, qb_ref, ka_ref, kb_ref, lng_ref, lnb_ref, ws_ref, bs_ref,
                    km_ref, vm_ref, bgate_ref, wbg_ref, wbm_ref,
                    qt_out, k_out, vt_out, part_out, g0_out):
    tm = x_ref.shape[0]
    xb = x_ref[...].astype(BF)

    zuv = _dot(xb, wall_ref[:, C_UV0:C_QM0])
    u = jax.nn.gelu(zuv[:, :GM_W])
    vb = _layer_norm(jax.nn.gelu(zuv[:, GM_W:]), lng_ref[...], lnb_ref[...]).astype(BF)
    nch = tm // GM_CHUNK
    sp_cols = []
    for g in range(GM_GROUPS):
        gsl = slice(g * GM_CHUNK, (g + 1) * GM_CHUNK)
        rhs = jnp.concatenate([vb[c * GM_CHUNK:(c + 1) * GM_CHUNK, gsl] for c in range(nch)], axis=1)
        spg = _dot(ws_ref[g], rhs)
        bias = bs_ref[g]
        sp_cols.append(jnp.concatenate(
            [spg[:, c * GM_CHUNK:(c + 1) * GM_CHUNK] + bias for c in range(nch)], axis=0))
    sp = jnp.concatenate(sp_cols, axis=1)
    pb = _dot((u * sp).astype(BF), wbg_ref[...])

    qm = _dot(xb, wall_ref[:, C_QM0:C_G0]).astype(BF)
    km, vm = km_ref[0], vm_ref[0]
    oc_cols = []
    for h in range(MEM_HEADS):
        hsl = slice(h * MEM_HEAD_DIM, (h + 1) * MEM_HEAD_DIM)
        s = _dot_nt(qm[:, hsl], km[:, hsl])
        e = jnp.exp2(s - jnp.max(s, axis=-1, keepdims=True))
        o = _dot(e.astype(BF), vm[:, hsl])
        oc_cols.append(o / jnp.sum(e, axis=-1, keepdims=True))
    pc = _dot(jnp.concatenate(oc_cols, axis=1).astype(BF), wbm_ref[...])

    def gate(j):
        zg = _dot(xb, wall_ref[:, C_G0 + j * D_MODEL:C_G0 + (j + 1) * D_MODEL])
        return jax.nn.sigmoid(zg + bgate_ref[j:j + 1, :])

    part = gate(1) * pb
    part = part + gate(2) * pc
    part_out[...] = part.astype(BF)
    g0_out[...] = gate(0).astype(BF)

    z1 = _dot(xb, wall_ref[:, C_Q0:C_UV0])
    cqn = _rms_norm(z1[:, C_Q0:C_KV0], gq_ref[...]).astype(BF)
    ckvn = _rms_norm(z1[:, C_KV0:C_KR0], gkv_ref[...]).astype(BF)
    k_rope = _rope(z1[:, C_KR0:C_UV0], ka_ref[...], kb_ref[...])
    qt_all = _dot_nt(wuqt_ref[...], cqn)
    k_all = _dot(ckvn, wuk_ref[...])
    qa, qb = qa_ref[...], qb_ref[...]
    for h in range(N_HEADS):
        sl = slice(h * HEAD_PAD, (h + 1) * HEAD_PAD)
        qt = qt_all[sl, :]
        rot = jnp.concatenate([qt[QK_ROPE:], qt[:QK_ROPE]], axis=0)
        qt_out[0, sl, :] = (qt * qa + rot * qb).astype(BF)
        k_out[:, sl] = (k_all[:, sl] + k_rope).astype(BF)
    vt_out[0] = _dot_nt(wuvt_ref[...], ckvn).astype(BF)


def _in_proj(x2d, km, vm, w, tabs, *, seq, tm):
    n = x2d.shape[0]
    tiles_per_seq = seq // tm
    row = lambda i: (i, 0)
    tab = lambda i: (i % tiles_per_seq, 0)
    tab_t = lambda i: (0, i % tiles_per_seq)
    seq_t = lambda i: (i // tiles_per_seq, 0, i % tiles_per_seq)
    memb = lambda i: (i // tiles_per_seq, 0, 0)
    m = km.shape[1]
    in_specs = [
        pl.BlockSpec((tm, D_MODEL), row),
        _const_spec((D_MODEL, C_END)),
        _const_spec((1, Q_LORA)),
        _const_spec((N_HEADS * HEAD_PAD, Q_LORA)),
        _const_spec((1, KV_LORA)),
        _const_spec((KV_LORA, N_HEADS * HEAD_PAD)),
        _const_spec((N_HEADS * V_HEAD, KV_LORA)),
        pl.BlockSpec((HEAD_PAD, tm), tab_t),
        pl.BlockSpec((HEAD_PAD, tm), tab_t),
        pl.BlockSpec((tm, HEAD_PAD), tab),
        pl.BlockSpec((tm, HEAD_PAD), tab),
        _const_spec((1, GM_W)),
        _const_spec((1, GM_W)),
        _const_spec((GM_GROUPS, GM_CHUNK, GM_CHUNK)),
        _const_spec((GM_GROUPS, GM_CHUNK, GM_CHUNK)),
        pl.BlockSpec((1, m, MEM_W), memb),
        pl.BlockSpec((1, m, MEM_W), memb),
        _const_spec((3, D_MODEL)),
        _const_spec((GM_W, D_MODEL)),
        _const_spec((MEM_W, D_MODEL)),
    ]
    out_specs = [
        pl.BlockSpec((1, N_HEADS * HEAD_PAD, tm), seq_t),
        pl.BlockSpec((tm, N_HEADS * HEAD_PAD), row),
        pl.BlockSpec((1, N_HEADS * V_HEAD, tm), lambda i: (i, 0, 0)),
        pl.BlockSpec((tm, D_MODEL), row),
        pl.BlockSpec((tm, D_MODEL), row),
    ]
    out_shape = [
        jax.ShapeDtypeStruct((n // seq, N_HEADS * HEAD_PAD, seq), BF),
        jax.ShapeDtypeStruct((n, N_HEADS * HEAD_PAD), BF),
        jax.ShapeDtypeStruct((n // tm, N_HEADS * V_HEAD, tm), BF),
        jax.ShapeDtypeStruct((n, D_MODEL), BF),
        jax.ShapeDtypeStruct((n, D_MODEL), BF),
    ]
    return pl.pallas_call(
        _in_proj_kernel,
        grid=(n // tm,),
        in_specs=in_specs,
        out_specs=out_specs,
        out_shape=out_shape,
        compiler_params=_params(1),
        name="in_proj",
    )(x2d, w["w_all"], w["g_q"], w["w_uqt"], w["g_kv"], w["w_uk"], w["w_uvt"],
      tabs["qa"], tabs["qb"], tabs["ka"], tabs["kb"], w["gm_ln_g"], w["gm_ln_b"],
      w["w_s"], w["b_s"], km, vm, w["b_gate"], w["w_br_gmlp"], w["w_br_mem"])


ATTN_HEADS_PER_STEP = 2
ATTN_KEY_CHUNK = 1024
ATTN_SCORE_PAD = 128


def _mla_attn_kernel(qt_ref, k_ref, vt_ref, o_ref, s_scr, *, nkc, tkc, tq):
    hg = ATTN_HEADS_PER_STEP
    n_sub = qt_ref.shape[2] // tq
    vt_per_chunk = tkc // vt_ref.shape[3]

    def scores(qs, c, slot, ms):
        start = pl.multiple_of(c * tkc, tkc)
        out = []
        for h in range(hg):
            ksl = slice(h * HEAD_PAD, (h + 1) * HEAD_PAD)
            s = _dot(k_ref[0, pl.ds(start, tkc), ksl], qt_ref[0, ksl, qs])
            s_scr[slot, h, :, :tq] = s
            out.append(jnp.maximum(ms[h], jnp.max(s, axis=0, keepdims=True)))
        return out

    def accumulate(c, slot, m_old, m_new, carry):
        es = [jnp.exp2(s_scr[slot, h, :, :tq] - m_new[h]) for h in range(hg)]
        out = []
        for h in range(hg):
            acc, l = carry[h]
            vsl = slice(h * V_HEAD, (h + 1) * V_HEAD)
            vt = jnp.concatenate([vt_ref[0, c * vt_per_chunk + i, vsl, :] for i in range(vt_per_chunk)],
                                 axis=1)
            alpha = jnp.exp2(m_old[h] - m_new[h])
            out.append((acc * alpha + _dot(vt, es[h].astype(BF)),
                        l * alpha + jnp.sum(es[h], axis=0, keepdims=True)))
        return out

    m_init = [jnp.full((1, tq), -jnp.inf, F32) for _ in range(hg)]
    carry_init = [(jnp.zeros((V_HEAD, tq), F32), jnp.zeros((1, tq), F32)) for _ in range(hg)]
    q_slices = [slice(i * tq, (i + 1) * tq) for i in range(n_sub)]
    m_first = scores(q_slices[0], 0, 0, m_init)
    for i, qs in enumerate(q_slices):

        def body(j, carry, qs=qs):
            m_a_old, m_a, accs = (list(t) for t in carry)
            a = 2 * j
            m_b = scores(qs, a + 1, 1, m_a)
            accs = accumulate(a, 0, m_a_old, m_a, accs)
            m_next = scores(qs, a + 2, 0, m_b)
            accs = accumulate(a + 1, 1, m_a, m_b, accs)
            return tuple(m_b), tuple(m_next), tuple(accs)

        m_a_old, m_a, accs = lax.fori_loop(0, nkc // 2 - 1, body,
                                           (tuple(m_init), tuple(m_first), tuple(carry_init)))
        a = nkc - 2
        m_b = scores(qs, a + 1, 1, list(m_a))
        accs = accumulate(a, 0, list(m_a_old), list(m_a), list(accs))
        if i + 1 < n_sub:
            m_first = scores(q_slices[i + 1], 0, 0, m_init)
        accs = accumulate(a + 1, 1, list(m_a), m_b, accs)
        for h in range(hg):
            acc, l = accs[h]
            o_ref[0, h * V_HEAD:(h + 1) * V_HEAD, qs] = (acc / l).astype(BF)


def _mla_attn(qt, k, vt, *, tq, n_sub):
    b, s, _ = k.shape
    n_vt, t_vt = vt.shape[1], vt.shape[3]
    tkc = ATTN_KEY_CHUNK if s % (2 * ATTN_KEY_CHUNK) == 0 else t_vt
    nkc = s // tkc
    hg = ATTN_HEADS_PER_STEP
    tq_step = tq * n_sub
    return pl.pallas_call(
        functools.partial(_mla_attn_kernel, nkc=nkc, tkc=tkc, tq=tq),
        grid=(b, N_HEADS // hg, s // tq_step),
        in_specs=[pl.BlockSpec((1, hg * HEAD_PAD, tq_step), lambda bi, g, qi: (bi, g, qi)),
                  pl.BlockSpec((1, s, hg * HEAD_PAD), lambda bi, g, qi: (bi, 0, g)),
                  pl.BlockSpec((1, n_vt, hg * V_HEAD, t_vt), lambda bi, g, qi: (bi, 0, g, 0))],
        out_specs=pl.BlockSpec((1, hg * V_HEAD, tq_step), lambda bi, g, qi: (bi, g, qi)),
        out_shape=jax.ShapeDtypeStruct((b, N_HEADS * V_HEAD, s), BF),
        scratch_shapes=[pltpu.VMEM((2, hg, tkc, tq + ATTN_SCORE_PAD), F32)],
        compiler_params=_params(3),
        name="mla_attn",
    )(qt, k, vt)


def _merge_kernel(ot_ref, g0_ref, part_ref, x_ref, wba_ref, wo_ref, g_ref, b_ref, y_ref):
    pa = _dot_tn(ot_ref[0], wba_ref[...])
    merged = g0_ref[...].astype(F32) * pa + part_ref[...].astype(F32)
    r = ALPHA * x_ref[...] + _dot(merged.astype(BF), wo_ref[...])
    y_ref[...] = _layer_norm(r, g_ref[...], b_ref[...])


def _merge(ot, g0, part, x2d, w, *, seq, tm):
    n = x2d.shape[0]
    tiles_per_seq = seq // tm
    row = lambda i: (i, 0)
    return pl.pallas_call(
        _merge_kernel,
        grid=(n // tm,),
        in_specs=[pl.BlockSpec((1, N_HEADS * V_HEAD, tm),
                               lambda i: (i // tiles_per_seq, 0, i % tiles_per_seq)),
                  pl.BlockSpec((tm, D_MODEL), row),
                  pl.BlockSpec((tm, D_MODEL), row),
                  pl.BlockSpec((tm, D_MODEL), row),
                  _const_spec((N_HEADS * V_HEAD, D_MODEL)),
                  _const_spec((D_MODEL, D_MODEL)),
                  _const_spec((1, D_MODEL)),
                  _const_spec((1, D_MODEL))],
        out_specs=pl.BlockSpec((tm, D_MODEL), row),
        out_shape=jax.ShapeDtypeStruct((n, D_MODEL), F32),
        compiler_params=_params(1),
        name="merge",
    )(ot, g0, part, x2d, w["w_br_mla"], w["w_o"], w["ln1_g"], w["ln1_b"])


FF_CHUNK = 256
HALO = 8


def _conv_ffn_kernel(y_ref, prev_ref, next_ref, win_ref, cw_ref, cb_ref, wout_ref, g_ref, b_ref,
                     o_ref, act_ref, *, tiles_per_seq):
    tm = y_ref.shape[0]
    i = pl.program_id(0)
    keep_prev = (i % tiles_per_seq != 0).astype(F32)
    keep_next = (i % tiles_per_seq != tiles_per_seq - 1).astype(F32)
    y = y_ref[...]
    yb = jnp.concatenate([prev_ref[...] * keep_prev, y, next_ref[...] * keep_next], axis=0).astype(BF)
    rows = tm + 2 * HALO

    def conv(hcol, c0):
        csl = slice(c0, c0 + FF_CHUNK)
        up = pltpu.roll(hcol, 1, 0)[HALO:HALO + tm]
        dn = pltpu.roll(hcol, rows - 1, 0)[HALO:HALO + tm]
        return (cb_ref[:, csl] + up * cw_ref[0:1, csl] + hcol[HALO:HALO + tm] * cw_ref[1:2, csl]
                + dn * cw_ref[2:3, csl])

    for j in range(D_FF // FF_CHUNK):
        a0, g0 = j * FF_CHUNK, D_FF + j * FF_CHUNK
        ca = conv(_dot(yb, win_ref[:, a0:a0 + FF_CHUNK]), a0)
        cg = conv(_dot(yb, win_ref[:, g0:g0 + FF_CHUNK]), g0)
        act_ref[:, a0:a0 + FF_CHUNK] = (jax.nn.silu(ca) * cg).astype(BF)
    half = tm // 2
    for r0 in (0, half):
        r = ALPHA * y[r0:r0 + half] + _dot(act_ref[r0:r0 + half, :], wout_ref[...])
        o_ref[r0:r0 + half, :] = _layer_norm(r, g_ref[...], b_ref[...])


def _conv_ffn(y2d, w, *, seq, tm):
    n = y2d.shape[0]
    tiles_per_seq = seq // tm
    nb = tm // HALO
    last = n // HALO - 1
    row = lambda i: (i, 0)
    return pl.pallas_call(
        functools.partial(_conv_ffn_kernel, tiles_per_seq=tiles_per_seq),
        grid=(n // tm,),
        in_specs=[pl.BlockSpec((tm, D_MODEL), row),
                  pl.BlockSpec((HALO, D_MODEL), lambda i: (jnp.maximum(i * nb - 1, 0), 0)),
                  pl.BlockSpec((HALO, D_MODEL), lambda i: (jnp.minimum((i + 1) * nb, last), 0)),
                  _const_spec((D_MODEL, 2 * D_FF)),
                  _const_spec((3, 2 * D_FF)),
                  _const_spec((1, 2 * D_FF)),
                  _const_spec((D_FF, D_MODEL)),
                  _const_spec((1, D_MODEL)),
                  _const_spec((1, D_MODEL))],
        out_specs=pl.BlockSpec((tm, D_MODEL), row),
        out_shape=jax.ShapeDtypeStruct((n, D_MODEL), F32),
        scratch_shapes=[pltpu.VMEM((tm, D_FF), BF)],
        compiler_params=_params(1),
        name="conv_ffn",
    )(y2d, y2d, y2d, w["w_ffn_in"], w["conv_w"], w["conv_b"], w["w_ffn_out"], w["ln2_g"], w["ln2_b"])


def _rope_tables(seq):
    pos = jnp.arange(seq, dtype=F32)
    inv = ROPE_THETA ** (-jnp.arange(0, QK_ROPE, 2, dtype=F32) / QK_ROPE)
    ang = pos[:, None] * inv[None, :]
    cos2 = jnp.tile(jnp.cos(ang), (1, 2))
    sin2 = jnp.tile(jnp.sin(ang), (1, 2))
    c = (QK_NOPE + QK_ROPE) ** -0.5 * LOG2E
    z_nope = jnp.zeros((seq, QK_NOPE), F32)
    z_tail = jnp.zeros((seq, HEAD_PAD - QK_NOPE - QK_ROPE), F32)
    return {
        "qa": jnp.concatenate([jnp.full((seq, QK_NOPE), c, F32), c * cos2, z_tail], axis=1).T,
        "qb": jnp.concatenate([z_nope, c * sin2, z_tail], axis=1).T,
        "ka": jnp.concatenate([z_nope, cos2, z_tail], axis=1),
        "kb": jnp.concatenate([z_nope, sin2, z_tail], axis=1),
    }


def _rot_cols(w_rope):
    half = QK_ROPE // 2
    return jnp.concatenate([-w_rope[..., half:], w_rope[..., :half]], axis=-1)


def _layer_weights(l, p):
    w_in = p["w_in"][l]
    offs = (0, Q_LORA, Q_LORA + KV_LORA, Q_LORA + KV_LORA + QK_ROPE)
    w_q, w_kv, w_kr = (w_in[:, offs[0]:offs[1]], w_in[:, offs[1]:offs[2]], w_in[:, offs[2]:offs[3]])
    rest = w_in[:, offs[3]:]
    w_kr_pad = jnp.concatenate([jnp.zeros((D_MODEL, ROPE_LANE), F32), w_kr, _rot_cols(w_kr)], axis=1)
    w_all = jnp.concatenate([w_q, w_kv, w_kr_pad, rest], axis=1).astype(BF)

    w_uq = p["w_uq"][l].reshape(Q_LORA, N_HEADS, QK_NOPE + QK_ROPE)
    w_uq = jnp.concatenate([w_uq, _rot_cols(w_uq[..., QK_NOPE:])], axis=-1)
    w_ukv = p["w_ukv"][l].reshape(KV_LORA, N_HEADS, QK_NOPE + V_HEAD)
    w_uk = jnp.concatenate([w_ukv[..., :QK_NOPE], jnp.zeros((KV_LORA, N_HEADS, HEAD_PAD - QK_NOPE), F32)],
                           axis=-1)
    w_uvt = w_ukv[..., QK_NOPE:].reshape(KV_LORA, N_HEADS * V_HEAD).T
    b_s = jnp.broadcast_to(p["b_s"][l][:, :, None], (GM_GROUPS, GM_CHUNK, GM_CHUNK))
    return {
        "w_all": w_all,
        "g_q": p["g_q"][l][None, :],
        "w_uqt": w_uq.reshape(Q_LORA, N_HEADS * HEAD_PAD).T.astype(BF),
        "g_kv": p["g_kv"][l][None, :],
        "w_uk": w_uk.reshape(KV_LORA, N_HEADS * HEAD_PAD).astype(BF),
        "w_uvt": w_uvt.astype(BF),
        "gm_ln_g": p["gm_ln_g"][l][None, :],
        "gm_ln_b": p["gm_ln_b"][l][None, :],
        "w_s": p["w_s"][l].astype(BF),
        "b_s": b_s,
        "w_mem_kv": p["w_mem_kv"][l].astype(BF),
        "b_gate": p["b_gate"][l].reshape(3, D_MODEL),
        "w_br_mla": p["w_br_mla"][l].astype(BF),
        "w_br_gmlp": p["w_br_gmlp"][l].astype(BF),
        "w_br_mem": p["w_br_mem"][l].astype(BF),
        "w_o": p["w_o"][l].astype(BF),
        "ln1_g": p["ln1_g"][l][None, :],
        "ln1_b": p["ln1_b"][l][None, :],
        "w_ffn_in": p["w_ffn_in"][l].astype(BF),
        "conv_w": p["conv_w"][l],
        "conv_b": p["conv_b"][l][None, :],
        "w_ffn_out": p["w_ffn_out"][l].astype(BF),
        "ln2_g": p["ln2_g"][l][None, :],
        "ln2_b": p["ln2_b"][l][None, :],
    }


def _tiles(seq):
    pick = lambda pref: max(t for t in (128, 256, 512) if t <= pref and seq % t == 0)
    tq = pick(512)
    n_sub = max(n for n in (1, 2, 4) if seq % (tq * n) == 0)
    return pick(512), tq, n_sub, pick(512), pick(512)


def _trunk(x, mem, weights):
    b, seq, _ = x.shape
    tm_a, tq, n_sub, tm_c, tm_f = _tiles(seq)
    tabs = _rope_tables(seq)
    x2d = x.reshape(b * seq, D_MODEL)
    for w in weights:
        km, vm = _mem_kv(mem, w["w_mem_kv"])
        qt, k, vt, part, g0 = _in_proj(x2d, km, vm, w, tabs, seq=seq, tm=tm_a)
        ot = _mla_attn(qt, k.reshape(b, seq, -1),
                       vt.reshape(b, seq // tm_a, N_HEADS * V_HEAD, tm_a), tq=tq, n_sub=n_sub)
        y = _merge(ot, g0, part, x2d, w, seq=seq, tm=tm_c)
        x2d = _conv_ffn(y, w, seq=seq, tm=tm_f)
    return x2d.reshape(b, seq, D_MODEL)


def kernel(x_prompt, x_sample, mem_prompt, mem_sample, w_in, g_q, w_uq, g_kv, w_ukv, gm_ln_g, gm_ln_b,
           w_s, b_s, w_mem_kv, b_gate, w_br_mla, w_br_gmlp, w_br_mem, w_o, ln1_g, ln1_b, w_ffn_in,
           conv_w, conv_b, w_ffn_out, ln2_g, ln2_b):
    p = dict(w_in=w_in, g_q=g_q, w_uq=w_uq, g_kv=g_kv, w_ukv=w_ukv, gm_ln_g=gm_ln_g, gm_ln_b=gm_ln_b,
             w_s=w_s, b_s=b_s, w_mem_kv=w_mem_kv, b_gate=b_gate, w_br_mla=w_br_mla,
             w_br_gmlp=w_br_gmlp, w_br_mem=w_br_mem, w_o=w_o, ln1_g=ln1_g, ln1_b=ln1_b,
             w_ffn_in=w_ffn_in, conv_w=conv_w, conv_b=conv_b, w_ffn_out=w_ffn_out, ln2_g=ln2_g,
             ln2_b=ln2_b)
    weights = [_layer_weights(l, p) for l in range(DEPTH)]
    return (_trunk(x_prompt, mem_prompt, weights), _trunk(x_sample, mem_sample, weights))
```
